```python
import jax, jax.numpy as jnp
from jax import lax
import numpy as np

D_MODEL = 1024
BATCH = 4
SEQ = 4096
DEPTH = 1

EPS = 1e-6
NEG = -1e30
CHUNK = 128
A_GROUPS = 4
A_WIDTH = D_MODEL
A_GROUP_W = A_WIDTH // A_GROUPS
B_PATTERNS = ((128, 1), (512, 4), (2048, 16))
B_NPAT = len(B_PATTERNS)
B_HEADS = 8
B_HEAD_DIM = 64
B_WIDTH = B_HEADS * B_HEAD_DIM
B_QKV = B_NPAT * 3 * B_WIDTH
N_IN = 3 * A_WIDTH + B_QKV + B_WIDTH + 2 * D_MODEL

kernel_name = "hybrid_gmlp_dilated_attn_gated_block"


def _rmsnorm(t, g):
    t32 = t.astype(jnp.float32)
    t32 = t32 * lax.rsqrt(jnp.mean(t32 * t32, axis=-1, keepdims=True) + EPS)
    return (t32 * g.astype(jnp.float32)).astype(t.dtype)


def _layernorm(t, g, b):
    t32 = t.astype(jnp.float32)
    mu = jnp.mean(t32, axis=-1, keepdims=True)
    var = jnp.mean(jnp.square(t32 - mu), axis=-1, keepdims=True)
    y = (t32 - mu) * lax.rsqrt(var + EPS) * g.astype(jnp.float32) + b.astype(jnp.float32)
    return y.astype(t.dtype)


def _gmlp_sgu(u, v, w_s, b_s, ln_g, ln_b):
    bsz, seq, _ = v.shape
    u = jax.nn.gelu(u)
    v = _layernorm(jax.nn.gelu(v), ln_g, ln_b)
    vc = v.reshape(bsz, seq // CHUNK, CHUNK, A_GROUPS, A_GROUP_W)
    mask = jnp.tril(jnp.ones((CHUNK, CHUNK), dtype=bool))
    w = jnp.where(mask[None], w_s, jnp.zeros_like(w_s))
    s = jnp.einsum('gij,bcjgh->bcigh', w, vc) + jnp.swapaxes(b_s, 0, 1)[None, None, :, :, None]
    return u * s.reshape(bsz, seq, A_WIDTH)


def _strided(t, d):
    bsz, seq = t.shape[:2]
    return jnp.swapaxes(t.reshape(bsz, seq // d, d, *t.shape[2:]), 1, 2)


def _unstrided(t, seq):
    bsz = t.shape[0]
    return jnp.swapaxes(t, 1, 2).reshape(bsz, seq, *t.shape[3:])


def _dilated_window(q, k, v, window, dilation):
    bsz, seq, nh, hd = q.shape
    n = window // dilation
    L = seq // dilation
    nb = -(-L // n)
    Lp = nb * n

    def blocks(t):
        t = _strided(t, dilation)
        t = jnp.pad(t, ((0, 0), (0, 0), (0, Lp - L), (0, 0), (0, 0)))
        return t.reshape(bsz, dilation, nb, n, nh, hd)

    def with_prev(t):
        prev = jnp.pad(t, ((0, 0), (0, 0), (1, 0), (0, 0), (0, 0), (0, 0)))[:, :, :-1]
        return jnp.concatenate([prev, t], axis=3)

    qb = blocks(q)
    kk = with_prev(blocks(k))
    vv = with_prev(blocks(v))
    s = jnp.einsum('bdcqhe,bdckhe->bdchqk', qb, kk).astype(jnp.float32) * (hd ** -0.5)
    qi = jnp.arange(n)[:, None]
    kj = jnp.arange(2 * n)[None, :]
    dist = qi + n - kj
    band = (dist >= 0) & (dist <= n)
    key_pos = jnp.arange(nb)[:, None] * n + jnp.arange(2 * n)[None, :] - n
    mask = band[None] & (key_pos >= 0)[:, None, :]
    s = jnp.where(mask[:, None], s, NEG)
    m = jnp.max(s, axis=-1, keepdims=True)
    e = jnp.exp(s - m)
    den = jnp.sum(e, axis=-1, keepdims=True)
    p = (e / den).astype(v.dtype)
    o = jnp.einsum('bdchqk,bdckhe->bdcqhe', p, vv)
    lse = jnp.swapaxes((m + jnp.log(den))[..., 0], 3, 4)
    o = o.reshape(bsz, dilation, Lp, nh, hd)[:, :, :L]
    lse = lse.reshape(bsz, dilation, Lp, nh)[:, :, :L]
    return _unstrided(o, seq), _unstrided(lse, seq)


def _dilated_mixture(qkv, qn_g, kn_g):
    outs, lses = [], []
    for p, (window, dilation) in enumerate(B_PATTERNS):
        q = _rmsnorm(qkv[:, :, p, 0], qn_g[p])
        k = _rmsnorm(qkv[:, :, p, 1], kn_g[p])
        o, lse = _dilated_window(q, k, qkv[:, :, p, 2], window, dilation)
        outs.append(o)
        lses.append(lse)
    wts = jax.nn.softmax(jnp.stack(lses, axis=0), axis=0)
    o = jnp.sum(wts[..., None].astype(qkv.dtype) * jnp.stack(outs, axis=0), axis=0)
    return o


def setup_inputs(seed: int = 0) -> dict:
    key = jax.random.key(seed)
    ks = jax.random.split(key, 12)
    f32 = jnp.float32
    x = jax.random.normal(ks[0], (BATCH, SEQ, D_MODEL), f32)
    norm_g = 1.0 + 0.02 * jax.random.normal(ks[1], (DEPTH, D_MODEL), f32)
    w_in = jax.random.normal(ks[2], (DEPTH, D_MODEL, N_IN), f32) * D_MODEL ** -0.5
    a_ws = jax.random.normal(ks[3], (DEPTH, A_GROUPS, CHUNK, CHUNK), f32) * CHUNK ** -0.5
    a_bs = 1.0 + 0.02 * jax.random.normal(ks[4], (DEPTH, A_GROUPS, CHUNK), f32)
    a_ln_g = 1.0 + 0.02 * jax.random.normal(ks[5], (DEPTH, A_WIDTH), f32)
    a_ln_b = 0.02 * jax.random.normal(ks[6], (DEPTH, A_WIDTH), f32)
    b_qn_g = 1.0 + 0.02 * jax.random.normal(ks[7], (DEPTH, B_NPAT, B_HEAD_DIM), f32)
    b_kn_g = 1.0 + 0.02 * jax.random.normal(ks[8], (DEPTH, B_NPAT, B_HEAD_DIM), f32)
    w_oa = jax.random.normal(ks[9], (DEPTH, A_WIDTH, D_MODEL), f32) * A_WIDTH ** -0.5
    w_ob = jax.random.normal(ks[10], (DEPTH, B_WIDTH, D_MODEL), f32) * B_WIDTH ** -0.5
    w_out = jax.random.normal(ks[11], (DEPTH, D_MODEL, D_MODEL), f32) * D_MODEL ** -0.5
    return {"x": x, "norm_g": norm_g, "w_in": w_in, "a_ws": a_ws, "a_bs": a_bs,
            "a_ln_g": a_ln_g, "a_ln_b": a_ln_b, "b_qn_g": b_qn_g, "b_kn_g": b_kn_g,
            "w_oa": w_oa, "w_ob": w_ob, "w_out": w_out}


def reference(x, norm_g, w_in, a_ws, a_bs, a_ln_g, a_ln_b, b_qn_g, b_kn_g, w_oa, w_ob, w_out):
    bsz, seq, _ = x.shape
    split_pts = np.cumsum([A_WIDTH, A_WIDTH, A_WIDTH, B_QKV, B_WIDTH, D_MODEL]).tolist()
    for l in range(DEPTH):
        h = _rmsnorm(x, norm_g[l])
        proj = jnp.einsum('bsd,dn->bsn', h, w_in[l])
        a_u, a_v, a_gate, b_qkv, b_gate, g_a, g_b = jnp.split(proj, split_pts, axis=-1)
        ya = _gmlp_sgu(a_u, a_v, a_ws[l], a_bs[l], a_ln_g[l], a_ln_b[l]) * jax.nn.silu(a_gate)
        ya = jnp.einsum('bsc,cd->bsd', ya, w_oa[l])
        qkv = b_qkv.reshape(bsz, seq, B_NPAT, 3, B_HEADS, B_HEAD_DIM)
        yb = _dilated_mixture(qkv, b_qn_g[l], b_kn_g[l]).reshape(bsz, seq, B_WIDTH)
        yb = jnp.einsum('bsc,cd->bsd', yb * jax.nn.silu(b_gate), w_ob[l])
        merged = jax.nn.sigmoid(g_a) * ya + jax.nn.sigmoid(g_b) * yb
        x = x + jnp.einsum('bsd,de->bse', merged, w_out[l])
    return x
```

```python
import functools

import jax
import jax.numpy as jnp
from jax import lax
from jax.experimental import pallas as pl
from jax.experimental.pallas import tpu as pltpu

F32 = jnp.float32
BF16 = jnp.bfloat16

EPS = 1e-6
NEG = -1e30
CHUNK = 128
A_GROUPS = 4
B_PATTERNS = ((128, 1), (512, 4), (2048, 16))
B_HEADS = 8
B_HEAD_DIM = 64
B_WIDTH = B_HEADS * B_HEAD_DIM
ATTN_BLOCK = 128

VMEM_LIMIT_BYTES = 48 * 1024 * 1024


def _params(*semantics):
    return pltpu.CompilerParams(dimension_semantics=semantics, vmem_limit_bytes=VMEM_LIMIT_BYTES)


def _inproj_kernel(x_ref, g_ref, w_ref, o_ref, h_ref):
    @pl.when(pl.program_id(1) == 0)
    def _():
        x = x_ref[...]
        ms = jnp.mean(x * x, axis=-1, keepdims=True)
        h_ref[...] = (x * lax.rsqrt(ms + EPS) * g_ref[...]).astype(BF16)

    o_ref[...] = jnp.dot(h_ref[...], w_ref[...], preferred_element_type=F32).astype(o_ref.dtype)


def _inproj(x2, g, w, *, tm=1024, tn=1024):
    t, d = x2.shape
    n = w.shape[1]
    return pl.pallas_call(
        _inproj_kernel,
        grid=(t // tm, n // tn),
        in_specs=[
            pl.BlockSpec((tm, d), lambda i, j: (i, 0)),
            pl.BlockSpec((1, d), lambda i, j: (0, 0)),
            pl.BlockSpec((d, tn), lambda i, j: (0, j)),
        ],
        out_specs=pl.BlockSpec((tm, tn), lambda i, j: (i, j)),
        out_shape=jax.ShapeDtypeStruct((t, n), BF16),
        scratch_shapes=[pltpu.VMEM((tm, d), BF16)],
        compiler_params=_params("parallel", "arbitrary"),
        name="inproj",
    )(x2, g, w)


def _branch_a_kernel(u_ref, v_ref, gate_ref, ga_ref, ws_ref, bias_ref, lng_ref, lnb_ref, woa_ref,
                     o_ref, s_ref):
    tm, aw = u_ref.shape
    gw = aw // A_GROUPS
    v = jax.nn.gelu(v_ref[...].astype(F32))
    mu = jnp.mean(v, axis=-1, keepdims=True)
    vc = v - mu
    var = jnp.mean(vc * vc, axis=-1, keepdims=True)
    vb = (vc * lax.rsqrt(var + EPS) * lng_ref[...] + lnb_ref[...]).astype(BF16)

    row = lax.broadcasted_iota(jnp.int32, (CHUNK, CHUNK), 0)
    col = lax.broadcasted_iota(jnp.int32, (CHUNK, CHUNK), 1)
    for g in range(A_GROUPS):
        w = jnp.where(row >= col, ws_ref[g], 0.0).astype(BF16)
        for c in range(tm // CHUNK):
            s_ref[c * CHUNK:(c + 1) * CHUNK, g * gw:(g + 1) * gw] = jnp.dot(
                w, vb[c * CHUNK:(c + 1) * CHUNK, g * gw:(g + 1) * gw], preferred_element_type=F32)
    bias = bias_ref[...]
    for c in range(tm // CHUNK):
        s_ref[c * CHUNK:(c + 1) * CHUNK, :] += bias

    u = jax.nn.gelu(u_ref[...].astype(F32))
    ya = (u * s_ref[...] * jax.nn.silu(gate_ref[...].astype(F32))).astype(BF16)
    ya = jnp.dot(ya, woa_ref[...], preferred_element_type=F32)
    o_ref[...] = (jax.nn.sigmoid(ga_ref[...].astype(F32)) * ya).astype(o_ref.dtype)


def _branch_a(proj, ws, bias, lng, lnb, woa, *, tm=512):
    t = proj.shape[0]
    aw = woa.shape[0]
    d = woa.shape[1]
    full = lambda shape: pl.BlockSpec(shape, lambda i: (0,) * len(shape))
    return pl.pallas_call(
        _branch_a_kernel,
        grid=(t // tm,),
        in_specs=[
            pl.BlockSpec((tm, aw), lambda i: (i, 0)),
            pl.BlockSpec((tm, aw), lambda i: (i, 1)),
            pl.BlockSpec((tm, aw), lambda i: (i, 2)),
            pl.BlockSpec((tm, d), lambda i: (i, 8)),
            full(ws.shape), full(bias.shape), full(lng.shape), full(lnb.shape), full(woa.shape),
        ],
        out_specs=pl.BlockSpec((tm, d), lambda i: (i, 0)),
        out_shape=jax.ShapeDtypeStruct((t, d), BF16),
        scratch_shapes=[pltpu.VMEM((tm, aw), F32)],
        compiler_params=_params("parallel"),
        name="branch_a",
    )(proj, proj, proj, proj, ws, bias, lng, lnb, woa)


def _rms(t, g):
    return t * lax.rsqrt(jnp.mean(t * t, axis=-1, keepdims=True) + EPS) * g


def _attn_kernel(q_ref, kp_ref, kc_ref, vp_ref, vc_ref, bias_ref, gq_ref, gk_ref, o_ref, lse_ref):
    bias = bias_ref[...]
    gq = gq_ref[...] * (B_HEAD_DIM ** -0.5)
    gk = gk_ref[...]
    for h in range(B_HEADS):
        sl = slice(h * B_HEAD_DIM, (h + 1) * B_HEAD_DIM)
        qh = _rms(q_ref[:, sl].astype(F32), gq).astype(BF16)
        kh = jnp.concatenate([_rms(kp_ref[:, sl].astype(F32), gk),
                              _rms(kc_ref[:, sl].astype(F32), gk)], axis=0).astype(BF16)
        vh = jnp.concatenate([vp_ref[:, sl], vc_ref[:, sl]], axis=0)
        s = lax.dot_general(qh, kh, (((1,), (1,)), ((), ())), preferred_element_type=F32) + bias
        m = jnp.max(s, axis=-1, keepdims=True)
        e = jnp.exp(s - m)
        den = jnp.sum(e, axis=-1, keepdims=True)
        o = jnp.dot(e.astype(BF16), vh, preferred_element_type=F32) / den
        o_ref[:, sl] = o.astype(o_ref.dtype)
        lse_ref[:, sl] = jnp.broadcast_to(m + jnp.log(den), (ATTN_BLOCK, B_HEAD_DIM))


def _attn(proj3, bias, gq, gk, *, pattern, dilation, n_in):
    bsz, length, _ = proj3.shape
    nb = length // ATTN_BLOCK
    per_row = n_in // B_WIDTH
    base = 6 + 3 * pattern
    blk = (None, ATTN_BLOCK, B_WIDTH)

    def cur(off):
        return pl.BlockSpec(blk, lambda b, r, c: (b, c, r * per_row + base + off))

    def prev(off):
        return pl.BlockSpec(blk, lambda b, r, c: (b, jnp.maximum(c - 1, 0), r * per_row + base + off))

    out_spec = pl.BlockSpec(blk, lambda b, r, c: (b, c, r))
    return pl.pallas_call(
        _attn_kernel,
        grid=(bsz, dilation, nb),
        in_specs=[
            cur(0), prev(1), cur(1), prev(2), cur(2),
            pl.BlockSpec((None, ATTN_BLOCK, 2 * ATTN_BLOCK), lambda b, r, c: (jnp.minimum(c, 1), 0, 0)),
            pl.BlockSpec((1, B_HEAD_DIM), lambda b, r, c: (0, 0)),
            pl.BlockSpec((1, B_HEAD_DIM), lambda b, r, c: (0, 0)),
        ],
        out_specs=[out_spec, out_spec],
        out_shape=[jax.ShapeDtypeStruct((bsz, length, dilation * B_WIDTH), BF16),
                   jax.ShapeDtypeStruct((bsz, length, dilation * B_WIDTH), F32)],
        compiler_params=_params("parallel", "parallel", "arbitrary"),
        name=f"attn_p{pattern}",
    )(proj3, proj3, proj3, proj3, proj3, bias, gq, gk)


def _band_bias():
    qi = lax.broadcasted_iota(jnp.int32, (ATTN_BLOCK, 2 * ATTN_BLOCK), 0)
    kj = lax.broadcasted_iota(jnp.int32, (ATTN_BLOCK, 2 * ATTN_BLOCK), 1)
    dist = qi + ATTN_BLOCK - kj
    band = (dist >= 0) & (dist <= ATTN_BLOCK)
    first = band & (kj >= ATTN_BLOCK)
    return jnp.stack([jnp.where(first, 0.0, NEG), jnp.where(band, 0.0, NEG)]).astype(F32)


def _merge_kernel(o0_ref, o1_ref, o2_ref, l0_ref, l1_ref, l2_ref, bg_ref, gb_ref, za_ref, x_ref,
                  wob_ref, wout_ref, out_ref):
    l0, l1, l2 = l0_ref[...], l1_ref[...], l2_ref[...]
    m = jnp.maximum(jnp.maximum(l0, l1), l2)
    e0, e1, e2 = jnp.exp(l0 - m), jnp.exp(l1 - m), jnp.exp(l2 - m)
    inv = 1.0 / (e0 + e1 + e2)
    yb = ((e0 * inv) * o0_ref[...].astype(F32) + (e1 * inv) * o1_ref[...].astype(F32)
          + (e2 * inv) * o2_ref[...].astype(F32))
    yb = (yb * jax.nn.silu(bg_ref[...].astype(F32))).astype(BF16)
    yb = jnp.dot(yb, wob_ref[...], preferred_element_type=F32)
    merged = za_ref[...].astype(F32) + jax.nn.sigmoid(gb_ref[...].astype(F32)) * yb
    out_ref[...] = x_ref[...] + jnp.dot(merged.astype(BF16), wout_ref[...], preferred_element_type=F32)


def _merge(outs, lses, proj, za, x2, wob, wout, *, tm=512):
    t, d = x2.shape
    bw = wob.shape[0]
    full = lambda shape: pl.BlockSpec(shape, lambda i: (0,) * len(shape))
    row = lambda w, j: pl.BlockSpec((tm, w), lambda i: (i, j))
    return pl.pallas_call(
        _merge_kernel,
        grid=(t // tm,),
        in_specs=[row(bw, 0)] * 6 + [
            row(bw, 15),
            row(d, 9),
            row(d, 0), row(d, 0),
            full(wob.shape), full(wout.shape),
        ],
        out_specs=row(d, 0),
        out_shape=jax.ShapeDtypeStruct((t, d), F32),
        compiler_params=_params("parallel"),
        name="merge",
    )(*outs, *lses, proj, proj, za, x2, wob, wout)


def kernel(x, norm_g, w_in, a_ws, a_bs, a_ln_g, a_ln_b, b_qn_g, b_kn_g, w_oa, w_ob, w_out):
    bsz, seq, d = x.shape
    depth = w_in.shape[0]
    n_in = w_in.shape[2]
    a_width = w_oa.shape[1]
    assert n_in == 3 * a_width + 3 * len(B_PATTERNS) * B_WIDTH + B_WIDTH + 2 * d
    t = bsz * seq
    band_bias = _band_bias()
    x2 = x.reshape(t, d)
    for l in range(depth):
        proj = _inproj(x2, norm_g[l].reshape(1, d), w_in[l].astype(BF16))
        sgu_bias = jnp.repeat(a_bs[l].T, a_width // A_GROUPS, axis=1)
        za = _branch_a(proj, a_ws[l], sgu_bias, a_ln_g[l].reshape(1, -1), a_ln_b[l].reshape(1, -1),
                       w_oa[l].astype(BF16))
        outs, lses = [], []
        for p, (window, dilation) in enumerate(B_PATTERNS):
            assert window // dilation == ATTN_BLOCK
            proj3 = proj.reshape(bsz, seq // dilation, dilation * n_in)
            o, lse = _attn(proj3, band_bias, b_qn_g[l, p].reshape(1, -1), b_kn_g[l, p].reshape(1, -1),
                           pattern=p, dilation=dilation, n_in=n_in)
            outs.append(o.reshape(t, B_WIDTH))
            lses.append(lse.reshape(t, B_WIDTH))
        x2 = _merge(outs, lses, proj, za, x2, w_ob[l].astype(BF16), w_out[l].astype(BF16))
    return x2.reshape(bsz, seq, d)
```

```python
import jax
import jax.numpy as jnp
from jax import lax
from jax.experimental import pallas as pl
from jax.experimental.pallas import tpu as pltpu

F32 = jnp.float32
BF16 = jnp.bfloat16

EPS = 1e-6
NEG = -1e30
CHUNK = 128
A_GROUPS = 4
B_PATTERNS = ((128, 1), (512, 4), (2048, 16))
B_HEADS = 8
B_HEAD_DIM = 64
B_WIDTH = B_HEADS * B_HEAD_DIM
LANES = 128
N_PAIRS = B_WIDTH // LANES
ATTN_BLOCK = 128
SPAN = 2048
UNITS = SPAN // ATTN_BLOCK

VMEM_LIMIT_BYTES = 56 * 1024 * 1024


def _params(*semantics):
    return pltpu.CompilerParams(dimension_semantics=semantics, vmem_limit_bytes=VMEM_LIMIT_BYTES)


def _resident(shape):
    return pl.BlockSpec(shape, lambda *_: (0,) * len(shape), pipeline_mode=pl.Buffered(1))


def _head_rms(t, gain):
    low = lax.broadcasted_iota(jnp.int32, (1, LANES), 1) < B_HEAD_DIM
    cols = []
    for cb in range(N_PAIRS):
        blk = t[:, cb * LANES:(cb + 1) * LANES]
        sq = blk * blk
        s0 = jnp.sum(jnp.where(low, sq, 0.0), axis=-1, keepdims=True)
        s1 = jnp.sum(jnp.where(low, 0.0, sq), axis=-1, keepdims=True)
        ms = jnp.where(low, s0, s1) * (1.0 / B_HEAD_DIM)
        cols.append(blk * lax.rsqrt(ms + EPS) * gain[:, cb * LANES:(cb + 1) * LANES])
    return jnp.concatenate(cols, axis=1)


def _emit_destrided(val, out_ref, t_ref, dilation):
    tm = val.shape[0]
    if dilation == 1:
        for hp in range(N_PAIRS):
            out_ref[hp, 0] = val[:, hp * LANES:(hp + 1) * LANES].astype(out_ref.dtype)
        return
    for hp in range(N_PAIRS):
        t_ref[hp] = val[:, hp * LANES:(hp + 1) * LANES]
    for hp in range(N_PAIRS):
        for r in range(dilation):
            out_ref[hp, r] = t_ref[hp, pl.ds(r, tm // dilation, stride=dilation), :].astype(out_ref.dtype)


def _fused_in_kernel(x_ref, ng_ref, win_ref, ws_ref, sb_ref, lng_ref, lnb_ref, woa_ref, gq_ref, gk_ref,
                     za_ref, gb_ref, bg_ref, q0_ref, k0_ref, v0_ref, q1_ref, k1_ref, v1_ref,
                     q2_ref, k2_ref, v2_ref, h_ref, s_ref, vb_ref, yp_ref, t_ref):
    tm, d_model = x_ref.shape
    a_width = woa_ref.shape[0]
    gw = a_width // A_GROUPS
    cbw = 512
    col_u, col_v, col_g = 0, a_width, 2 * a_width
    col_qkv = 3 * a_width
    col_bg = col_qkv + 3 * len(B_PATTERNS) * B_WIDTH
    col_ga = col_bg + B_WIDTH
    col_gb = col_ga + d_model

    x = x_ref[...]
    ms = jnp.mean(x * x, axis=-1, keepdims=True)
    h_ref[...] = (x * lax.rsqrt(ms + EPS) * ng_ref[...]).astype(BF16)

    def proj(col, width):
        return jnp.dot(h_ref[...], win_ref[:, col:col + width], preferred_element_type=F32)

    for cb in range(a_width // cbw):
        s_ref[:, cb * cbw:(cb + 1) * cbw] = jax.nn.gelu(proj(col_v + cb * cbw, cbw))
    v = s_ref[...]
    mu = jnp.mean(v, axis=-1, keepdims=True)
    vc = v - mu
    var = jnp.mean(vc * vc, axis=-1, keepdims=True)
    vb_ref[...] = (vc * lax.rsqrt(var + EPS) * lng_ref[...] + lnb_ref[...]).astype(BF16)
    row = lax.broadcasted_iota(jnp.int32, (CHUNK, CHUNK), 0)
    col = lax.broadcasted_iota(jnp.int32, (CHUNK, CHUNK), 1)
    for g in range(A_GROUPS):
        w = jnp.where(row >= col, ws_ref[g], 0.0).astype(BF16)
        gs = slice(g * gw, (g + 1) * gw)
        for c in range(tm // CHUNK):
            rs = slice(c * CHUNK, (c + 1) * CHUNK)
            s_ref[rs, gs] = jnp.dot(w, vb_ref[rs, gs], preferred_element_type=F32) + sb_ref[:, gs]
    for cb in range(a_width // cbw):
        cs = slice(cb * cbw, (cb + 1) * cbw)
        u = jax.nn.gelu(proj(col_u + cb * cbw, cbw))
        gate = jax.nn.silu(proj(col_g + cb * cbw, cbw))
        yp_ref[:, cs] = (u * s_ref[:, cs] * gate).astype(BF16)
    for cb in range(d_model // cbw):
        cs = slice(cb * cbw, (cb + 1) * cbw)
        ya = jnp.dot(yp_ref[...], woa_ref[:, cs], preferred_element_type=F32)
        za_ref[:, cs] = (jax.nn.sigmoid(proj(col_ga + cb * cbw, cbw)) * ya).astype(za_ref.dtype)
        gb_ref[:, cs] = jax.nn.sigmoid(proj(col_gb + cb * cbw, cbw)).astype(gb_ref.dtype)

    bg = jax.nn.silu(proj(col_bg, B_WIDTH))
    for hp in range(N_PAIRS):
        bg_ref[hp] = bg[:, hp * LANES:(hp + 1) * LANES].astype(bg_ref.dtype)
    outs = ((q0_ref, k0_ref, v0_ref), (q1_ref, k1_ref, v1_ref), (q2_ref, k2_ref, v2_ref))
    for p, (_, dilation) in enumerate(B_PATTERNS):
        q_ref, k_ref, v_ref = outs[p]
        base = col_qkv + p * 3 * B_WIDTH
        q = _head_rms(proj(base, B_WIDTH), gq_ref[p] * (B_HEAD_DIM ** -0.5))
        _emit_destrided(q, q_ref, t_ref, dilation)
        k = _head_rms(proj(base + B_WIDTH, B_WIDTH), gk_ref[p])
        _emit_destrided(k, k_ref, t_ref, dilation)
        _emit_destrided(proj(base + 2 * B_WIDTH, B_WIDTH), v_ref, t_ref, dilation)


def _fused_in(x2, ng, win, ws, sb, lng, lnb, woa, gq, gk, *, bsz, seq, tm=512):
    t, d = x2.shape
    a_width = woa.shape[0]
    tiles_per_seq = seq // tm
    row = lambda width: pl.BlockSpec((tm, width), lambda i: (i, 0))
    out_specs = [row(d), row(d), pl.BlockSpec((N_PAIRS, tm, LANES), lambda i: (0, i, 0))]
    out_shape = [jax.ShapeDtypeStruct((t, d), BF16), jax.ShapeDtypeStruct((t, d), BF16),
                 jax.ShapeDtypeStruct((N_PAIRS, t, LANES), BF16)]
    for _, dil in B_PATTERNS:
        spec = pl.BlockSpec((None, N_PAIRS, dil, tm // dil, LANES),
                            lambda i: (i // tiles_per_seq, 0, 0, i % tiles_per_seq, 0))
        shape = jax.ShapeDtypeStruct((bsz, N_PAIRS, dil, seq // dil, LANES), BF16)
        out_specs += [spec] * 3
        out_shape += [shape] * 3
    return pl.pallas_call(
        _fused_in_kernel,
        grid=(t // tm,),
        in_specs=[row(d), _resident(ng.shape), _resident(win.shape), _resident(ws.shape),
                  _resident(sb.shape), _resident(lng.shape), _resident(lnb.shape), _resident(woa.shape),
                  _resident(gq.shape), _resident(gk.shape)],
        out_specs=out_specs,
        out_shape=out_shape,
        scratch_shapes=[pltpu.VMEM((tm, d), BF16), pltpu.VMEM((tm, a_width), F32),
                        pltpu.VMEM((tm, a_width), BF16), pltpu.VMEM((tm, a_width), BF16),
                        pltpu.VMEM((N_PAIRS, tm, LANES), F32)],
        compiler_params=_params("parallel"),
        name="fused_in",
    )(x2, ng, win, ws, sb, lng, lnb, woa, gq, gk)


def _attn_unit(q2, k2, v2, bias, low):
    v2e = jnp.concatenate([v2, jnp.ones_like(v2)], axis=1)
    res = []
    for a in range(2):
        keep = low if a == 0 else jnp.logical_not(low)
        qa = jnp.where(keep, q2, jnp.zeros_like(q2))
        s = lax.dot_general(qa, k2, (((1,), (1,)), ((), ())), preferred_element_type=F32) + bias
        m = jnp.max(s, axis=-1, keepdims=True)
        e = jnp.exp(s - m).astype(BF16)
        oe = jnp.dot(e, v2e, preferred_element_type=F32)
        res.append((oe[:, :LANES], oe[:, LANES:], jnp.broadcast_to(m, (ATTN_BLOCK, LANES))))
    return tuple(jnp.where(low, res[0][i], res[1][i]) for i in range(3))


def _attn_kernel(q0_ref, k0_ref, v0_ref, kp0_ref, vp0_ref, q1_ref, k1_ref, v1_ref, kp1_ref, vp1_ref,
                 q2_ref, k2_ref, v2_ref, kp2_ref, vp2_ref, bias_ref, bg_ref, o_ref,
                 kf0_ref, vf0_ref, kf1_ref, vf1_ref, kf2_ref, vf2_ref, num_ref, den_ref, max_ref):
    first_span = pl.program_id(1) == 0
    low = lax.broadcasted_iota(jnp.int32, (1, LANES), 1) < B_HEAD_DIM
    pats = ((q0_ref, k0_ref, v0_ref, kp0_ref, vp0_ref, kf0_ref, vf0_ref),
            (q1_ref, k1_ref, v1_ref, kp1_ref, vp1_ref, kf1_ref, vf1_ref),
            (q2_ref, k2_ref, v2_ref, kp2_ref, vp2_ref, kf2_ref, vf2_ref))
    inner = 4
    for p, (_, dil) in enumerate(B_PATTERNS):
        q_ref, k_ref, v_ref, kp_ref, vp_ref, kf_ref, vf_ref = pats[p]
        nblk = UNITS // dil
        kf_ref[:, :ATTN_BLOCK, :] = kp_ref[...]
        kf_ref[:, ATTN_BLOCK:, :] = k_ref[...]
        vf_ref[:, :ATTN_BLOCK, :] = vp_ref[...]
        vf_ref[:, ATTN_BLOCK:, :] = v_ref[...]

        def body(it, carry, q_ref=q_ref, kf_ref=kf_ref, vf_ref=vf_ref, nblk=nblk, dil=dil, p=p):
            for j in range(inner):
                if dil == 1:
                    r, c = 0, it * inner + j
                elif nblk == inner:
                    r, c = it, j
                else:
                    r, c = it * (inner // nblk) + j // nblk, j % nblk
                row0 = c * ATTN_BLOCK if isinstance(c, int) else pl.multiple_of(c * ATTN_BLOCK, ATTN_BLOCK)
                q2 = q_ref[r, pl.ds(row0, ATTN_BLOCK), :]
                k2 = kf_ref[r, pl.ds(row0, 2 * ATTN_BLOCK), :]
                v2 = vf_ref[r, pl.ds(row0, 2 * ATTN_BLOCK), :]
                no_prev = jnp.logical_and(first_span, c == 0)
                bias = bias_ref[jnp.where(no_prev, 0, 1)]
                num, den, mx = _attn_unit(q2, k2, v2, bias, low)
                dst = pl.ds(c * (ATTN_BLOCK * dil) + r, ATTN_BLOCK, stride=dil)
                num_ref[p, dst, :] = num
                den_ref[p, dst, :] = den
                max_ref[p, dst, :] = mx
            return carry

        lax.fori_loop(0, UNITS // inner, body, 0)

    rows = 256
    for i in range(SPAN // rows):
        rs = slice(i * rows, (i + 1) * rows)
        m0, m1, m2 = max_ref[0, rs], max_ref[1, rs], max_ref[2, rs]
        mm = jnp.maximum(jnp.maximum(m0, m1), m2)
        a0, a1, a2 = jnp.exp(m0 - mm), jnp.exp(m1 - mm), jnp.exp(m2 - mm)
        num = a0 * num_ref[0, rs] + a1 * num_ref[1, rs] + a2 * num_ref[2, rs]
        den = a0 * den_ref[0, rs] + a1 * den_ref[1, rs] + a2 * den_ref[2, rs]
        o_ref[rs, :] = (num / den * bg_ref[rs, :].astype(F32)).astype(o_ref.dtype)


def _attn(qkv, bias, bg, *, bsz, seq):
    spans = seq // SPAN
    t = bsz * seq
    in_specs, scratch = [], []
    for _, dil in B_PATTERNS:
        rows = SPAN // dil
        cur = pl.BlockSpec((None, None, dil, rows, LANES), lambda b, c, hp: (b, hp, 0, c, 0))
        prev = pl.BlockSpec((None, None, dil, ATTN_BLOCK, LANES),
                            lambda b, c, hp, rows=rows: (b, hp, 0, jnp.maximum(c * (rows // ATTN_BLOCK) - 1, 0), 0))
        in_specs += [cur, cur, cur, prev, prev]
        scratch += [pltpu.VMEM((dil, rows + ATTN_BLOCK, LANES), BF16)] * 2
    span_spec = pl.BlockSpec((None, SPAN, LANES), lambda b, c, hp: (hp, b * spans + c, 0))
    in_specs += [pl.BlockSpec(bias.shape, lambda b, c, hp: (0, 0, 0)), span_spec]
    scratch += [pltpu.VMEM((len(B_PATTERNS), SPAN, LANES), F32)] * 3
    args = []
    for q, k, v in qkv:
        args += [q, k, v, k, v]
    return pl.pallas_call(
        _attn_kernel,
        grid=(bsz, spans, N_PAIRS),
        in_specs=in_specs,
        out_specs=span_spec,
        out_shape=jax.ShapeDtypeStruct((N_PAIRS, t, LANES), BF16),
        scratch_shapes=scratch,
        compiler_params=_params("parallel", "parallel", "parallel"),
        name="attn",
    )(*args, bias, bg)


def _band_bias():
    qi = lax.broadcasted_iota(jnp.int32, (ATTN_BLOCK, 2 * ATTN_BLOCK), 0)
    kj = lax.broadcasted_iota(jnp.int32, (ATTN_BLOCK, 2 * ATTN_BLOCK), 1)
    dist = qi + ATTN_BLOCK - kj
    band = (dist >= 0) & (dist <= ATTN_BLOCK)
    first = band & (kj >= ATTN_BLOCK)
    return jnp.stack([jnp.where(first, 0.0, NEG), jnp.where(band, 0.0, NEG)]).astype(F32)


def _out_kernel(yb_ref, gb_ref, za_ref, x_ref, wob_ref, wout_ref, out_ref):
    yb = jnp.concatenate([yb_ref[hp] for hp in range(N_PAIRS)], axis=1)
    yb = jnp.dot(yb, wob_ref[...], preferred_element_type=F32)
    merged = za_ref[...].astype(F32) + gb_ref[...].astype(F32) * yb
    out_ref[...] = x_ref[...] + jnp.dot(merged.astype(BF16), wout_ref[...], preferred_element_type=F32)


def _out(yb, gb, za, x2, wob, wout, *, tm=512):
    t, d = x2.shape
    row = pl.BlockSpec((tm, d), lambda i: (i, 0))
    return pl.pallas_call(
        _out_kernel,
        grid=(t // tm,),
        in_specs=[pl.BlockSpec((N_PAIRS, tm, LANES), lambda i: (0, i, 0)), row, row, row,
                  _resident(wob.shape), _resident(wout.shape)],
        out_specs=row,
        out_shape=jax.ShapeDtypeStruct((t, d), F32),
        compiler_params=_params("parallel"),
        name="out",
    )(yb, gb, za, x2, wob, wout)


def kernel(x, norm_g, w_in, a_ws, a_bs, a_ln_g, a_ln_b, b_qn_g, b_kn_g, w_oa, w_ob, w_out):
    bsz, seq, d = x.shape
    depth = w_in.shape[0]
    a_width = w_oa.shape[1]
    npat = len(B_PATTERNS)
    assert w_in.shape[2] == 3 * a_width + 3 * npat * B_WIDTH + B_WIDTH + 2 * d
    assert all(w // dil == ATTN_BLOCK and SPAN % (ATTN_BLOCK * dil) == 0 for w, dil in B_PATTERNS)
    assert seq % SPAN == 0
    t = bsz * seq
    band_bias = _band_bias()
    x2 = x.reshape(t, d)
    for l in range(depth):
        sgu_bias = jnp.repeat(a_bs[l].T, a_width // A_GROUPS, axis=1)
        gq = jnp.tile(b_qn_g[l], (1, B_HEADS)).reshape(npat, 1, B_WIDTH)
        gk = jnp.tile(b_kn_g[l], (1, B_HEADS)).reshape(npat, 1, B_WIDTH)
        res = _fused_in(x2, norm_g[l].reshape(1, d), w_in[l].astype(BF16), a_ws[l], sgu_bias,
                        a_ln_g[l].reshape(1, -1), a_ln_b[l].reshape(1, -1), w_oa[l].astype(BF16),
                        gq, gk, bsz=bsz, seq=seq)
        za, gb, bg = res[:3]
        qkv = [res[3 + 3 * p:6 + 3 * p] for p in range(npat)]
        yb = _attn(qkv, band_bias, bg, bsz=bsz, seq=seq)
        x2 = _out(yb, gb, za, x2, w_ob[l].astype(BF16), w_out[l].astype(BF16))
    return x2.reshape(bsz, seq, d)
```

```python
import jax
import jax.numpy as jnp
from jax import lax
from jax.experimental import pallas as pl
from jax.experimental.pallas import tpu as pltpu

F32 = jnp.float32
BF16 = jnp.bfloat16

EPS = 1e-6
NEG = -1e30
CHUNK = 128
A_GROUPS = 4
B_PATTERNS = ((128, 1), (512, 4), (2048, 16))
B_HEADS = 8
B_HEAD_DIM = 64
B_WIDTH = B_HEADS * B_HEAD_DIM
LANES = 128
N_PAIRS = B_WIDTH // LANES
ATTN_BLOCK = 128
SPAN = 2048
UNITS = SPAN // ATTN_BLOCK

VMEM_LIMIT_BYTES = 56 * 1024 * 1024


def _params(*semantics):
    return pltpu.CompilerParams(dimension_semantics=semantics, vmem_limit_bytes=VMEM_LIMIT_BYTES)


def _resident(shape):
    return pl.BlockSpec(shape, lambda *_: (0,) * len(shape), pipeline_mode=pl.Buffered(1))


def _head_rms(t, gain):
    low = lax.broadcasted_iota(jnp.int32, (1, LANES), 1) < B_HEAD_DIM
    cols = []
    for cb in range(N_PAIRS):
        blk = t[:, cb * LANES:(cb + 1) * LANES]
        sq = blk * blk
        s0 = jnp.sum(jnp.where(low, sq, 0.0), axis=-1, keepdims=True)
        s1 = jnp.sum(jnp.where(low, 0.0, sq), axis=-1, keepdims=True)
        ms = jnp.where(low, s0, s1) * (1.0 / B_HEAD_DIM)
        cols.append(blk * lax.rsqrt(ms + EPS) * gain[:, cb * LANES:(cb + 1) * LANES])
    return jnp.concatenate(cols, axis=1)


def _emit_destrided(val, out_ref, t_ref, dilation):
    tm = val.shape[0]
    if dilation == 1:
        for hp in range(N_PAIRS):
            out_ref[hp, 0] = val[:, hp * LANES:(hp + 1) * LANES].astype(out_ref.dtype)
        return
    for hp in range(N_PAIRS):
        t_ref[hp] = val[:, hp * LANES:(hp + 1) * LANES]
    for hp in range(N_PAIRS):
        for r in range(dilation):
            out_ref[hp, r] = t_ref[hp, pl.ds(r, tm // dilation, stride=dilation), :].astype(out_ref.dtype)


def _fused_in_kernel(x_ref, ng_ref, win_ref, ws_ref, sb_ref, lng_ref, lnb_ref, woa_ref, gq_ref, gk_ref,
                     za_ref, gb_ref, bg_ref, q0_ref, k0_ref, v0_ref, q1_ref, k1_ref, v1_ref,
                     q2_ref, k2_ref, v2_ref, h_ref, s_ref, vb_ref, yp_ref, t_ref):
    tm, d_model = x_ref.shape
    a_width = woa_ref.shape[0]
    gw = a_width // A_GROUPS
    cbw = 512
    col_u, col_v, col_g = 0, a_width, 2 * a_width
    col_qkv = 3 * a_width
    col_bg = col_qkv + 3 * len(B_PATTERNS) * B_WIDTH
    col_ga = col_bg + B_WIDTH
    col_gb = col_ga + d_model

    x = x_ref[...]
    ms = jnp.mean(x * x, axis=-1, keepdims=True)
    h_ref[...] = (x * lax.rsqrt(ms + EPS) * ng_ref[...]).astype(BF16)

    def proj(col, width):
        return jnp.dot(h_ref[...], win_ref[:, col:col + width], preferred_element_type=F32)

    for cb in range(a_width // cbw):
        s_ref[:, cb * cbw:(cb + 1) * cbw] = jax.nn.gelu(proj(col_v + cb * cbw, cbw))
    v = s_ref[...]
    mu = jnp.mean(v, axis=-1, keepdims=True)
    vc = v - mu
    var = jnp.mean(vc * vc, axis=-1, keepdims=True)
    vb_ref[...] = (vc * lax.rsqrt(var + EPS) * lng_ref[...] + lnb_ref[...]).astype(BF16)
    row = lax.broadcasted_iota(jnp.int32, (CHUNK, CHUNK), 0)
    col = lax.broadcasted_iota(jnp.int32, (CHUNK, CHUNK), 1)
    for g in range(A_GROUPS):
        w = jnp.where(row >= col, ws_ref[g], 0.0).astype(BF16)
        gs = slice(g * gw, (g + 1) * gw)
        for c in range(tm // CHUNK):
            rs = slice(c * CHUNK, (c + 1) * CHUNK)
            s_ref[rs, gs] = jnp.dot(w, vb_ref[rs, gs], preferred_element_type=F32) + sb_ref[:, gs]
    for cb in range(a_width // cbw):
        cs = slice(cb * cbw, (cb + 1) * cbw)
        u = jax.nn.gelu(proj(col_u + cb * cbw, cbw))
        gate = jax.nn.silu(proj(col_g + cb * cbw, cbw))
        yp_ref[:, cs] = (u * s_ref[:, cs] * gate).astype(BF16)
    for cb in range(d_model // cbw):
        cs = slice(cb * cbw, (cb + 1) * cbw)
        ya = jnp.dot(yp_ref[...], woa_ref[:, cs], preferred_element_type=F32)
        za_ref[:, cs] = (jax.nn.sigmoid(proj(col_ga + cb * cbw, cbw)) * ya).astype(za_ref.dtype)
        gb_ref[:, cs] = jax.nn.sigmoid(proj(col_gb + cb * cbw, cbw)).astype(gb_ref.dtype)

    bg = jax.nn.silu(proj(col_bg, B_WIDTH))
    for hp in range(N_PAIRS):
        bg_ref[hp] = bg[:, hp * LANES:(hp + 1) * LANES].astype(bg_ref.dtype)
    outs = ((q0_ref, k0_ref, v0_ref), (q1_ref, k1_ref, v1_ref), (q2_ref, k2_ref, v2_ref))
    for p, (_, dilation) in enumerate(B_PATTERNS):
        q_ref, k_ref, v_ref = outs[p]
        base = col_qkv + p * 3 * B_WIDTH
        q = _head_rms(proj(base, B_WIDTH), gq_ref[p] * (B_HEAD_DIM ** -0.5))
        _emit_destrided(q, q_ref, t_ref, dilation)
        k = _head_rms(proj(base + B_WIDTH, B_WIDTH), gk_ref[p])
        _emit_destrided(k, k_ref, t_ref, dilation)
        _emit_destrided(proj(base + 2 * B_WIDTH, B_WIDTH), v_ref, t_ref, dilation)


def _fused_in(x2, ng, win, ws, sb, lng, lnb, woa, gq, gk, *, bsz, seq, tm=512):
    t, d = x2.shape
    a_width = woa.shape[0]
    tiles_per_seq = seq // tm
    row = lambda width: pl.BlockSpec((tm, width), lambda i: (i, 0))
    out_specs = [row(d), row(d), pl.BlockSpec((N_PAIRS, tm, LANES), lambda i: (0, i, 0))]
    out_shape = [jax.ShapeDtypeStruct((t, d), BF16), jax.ShapeDtypeStruct((t, d), BF16),
                 jax.ShapeDtypeStruct((N_PAIRS, t, LANES), BF16)]
    for _, dil in B_PATTERNS:
        spec = pl.BlockSpec((None, N_PAIRS, dil, tm // dil, LANES),
                            lambda i: (i // tiles_per_seq, 0, 0, i % tiles_per_seq, 0))
        shape = jax.ShapeDtypeStruct((bsz, N_PAIRS, dil, seq // dil, LANES), BF16)
        out_specs += [spec] * 3
        out_shape += [shape] * 3
    return pl.pallas_call(
        _fused_in_kernel,
        grid=(t // tm,),
        in_specs=[row(d), _resident(ng.shape), _resident(win.shape), _resident(ws.shape),
                  _resident(sb.shape), _resident(lng.shape), _resident(lnb.shape), _resident(woa.shape),
                  _resident(gq.shape), _resident(gk.shape)],
        out_specs=out_specs,
        out_shape=out_shape,
        scratch_shapes=[pltpu.VMEM((tm, d), BF16), pltpu.VMEM((tm, a_width), F32),
                        pltpu.VMEM((tm, a_width), BF16), pltpu.VMEM((tm, a_width), BF16),
                        pltpu.VMEM((N_PAIRS, tm, LANES), F32)],
        compiler_params=_params("parallel"),
        name="fused_in",
    )(x2, ng, win, ws, sb, lng, lnb, woa, gq, gk)


def _attn_unit(q2, k2, v2, bias, low):
    v2e = jnp.concatenate([v2, jnp.ones_like(v2)], axis=1)
    res = []
    for a in range(2):
        keep = low if a == 0 else jnp.logical_not(low)
        qa = jnp.where(keep, q2, jnp.zeros_like(q2))
        s = lax.dot_general(qa, k2, (((1,), (1,)), ((), ())), preferred_element_type=F32) + bias
        m = jnp.max(s, axis=-1, keepdims=True)
        e = jnp.exp(s - m).astype(BF16)
        oe = jnp.dot(e, v2e, preferred_element_type=F32)
        res.append((oe[:, :LANES], oe[:, LANES:], jnp.broadcast_to(m, (ATTN_BLOCK, LANES))))
    return tuple(jnp.where(low, res[0][i], res[1][i]) for i in range(3))


def _attn_kernel(q0_ref, k0_ref, v0_ref, kp0_ref, vp0_ref, q1_ref, k1_ref, v1_ref, kp1_ref, vp1_ref,
                 q2_ref, k2_ref, v2_ref, kp2_ref, vp2_ref, bias_ref, bg_ref, o_ref,
                 num_ref, den_ref, max_ref):
    first_span = pl.program_id(1) == 0
    low = lax.broadcasted_iota(jnp.int32, (1, LANES), 1) < B_HEAD_DIM
    pats = ((q0_ref, k0_ref, v0_ref, kp0_ref, vp0_ref),
            (q1_ref, k1_ref, v1_ref, kp1_ref, vp1_ref),
            (q2_ref, k2_ref, v2_ref, kp2_ref, vp2_ref))
    for p, (_, dil) in enumerate(B_PATTERNS):
        q_ref, k_ref, v_ref, kp_ref, vp_ref = pats[p]
        for r in range(dil):
            for c in range(UNITS // dil):
                rows = slice(c * ATTN_BLOCK, (c + 1) * ATTN_BLOCK)
                if c == 0:
                    k2 = jnp.concatenate([kp_ref[r], k_ref[r, rows, :]], axis=0)
                    v2 = jnp.concatenate([vp_ref[r], v_ref[r, rows, :]], axis=0)
                    bias = bias_ref[jnp.where(first_span, 0, 1)]
                else:
                    both = slice((c - 1) * ATTN_BLOCK, (c + 1) * ATTN_BLOCK)
                    k2, v2 = k_ref[r, both, :], v_ref[r, both, :]
                    bias = bias_ref[1]
                num, den, mx = _attn_unit(q_ref[r, rows, :], k2, v2, bias, low)
                dst = pl.ds(c * (ATTN_BLOCK * dil) + r, ATTN_BLOCK, stride=dil)
                num_ref[p, dst, :] = num
                den_ref[p, dst, :] = den
                max_ref[p, dst, :] = mx

    rows = 256
    for i in range(SPAN // rows):
        rs = slice(i * rows, (i + 1) * rows)
        m0, m1, m2 = max_ref[0, rs], max_ref[1, rs], max_ref[2, rs]
        mm = jnp.maximum(jnp.maximum(m0, m1), m2)
        a0, a1, a2 = jnp.exp(m0 - mm), jnp.exp(m1 - mm), jnp.exp(m2 - mm)
        num = a0 * num_ref[0, rs] + a1 * num_ref[1, rs] + a2 * num_ref[2, rs]
        den = a0 * den_ref[0, rs] + a1 * den_ref[1, rs] + a2 * den_ref[2, rs]
        o_ref[rs, :] = (num / den * bg_ref[rs, :].astype(F32)).astype(o_ref.dtype)


def _attn(qkv, bias, bg, *, bsz, seq):
    spans = seq // SPAN
    t = bsz * seq
    in_specs, scratch = [], []
    for _, dil in B_PATTERNS:
        rows = SPAN // dil
        cur = pl.BlockSpec((None, None, dil, rows, LANES), lambda b, c, hp: (b, hp, 0, c, 0))
        prev = pl.BlockSpec((None, None, dil, ATTN_BLOCK, LANES),
                            lambda b, c, hp, rows=rows: (b, hp, 0, jnp.maximum(c * (rows // ATTN_BLOCK) - 1, 0), 0))
        in_specs += [cur, cur, cur, prev, prev]
    span_spec = pl.BlockSpec((None, SPAN, LANES), lambda b, c, hp: (hp, b * spans + c, 0))
    in_specs += [pl.BlockSpec(bias.shape, lambda b, c, hp: (0, 0, 0)), span_spec]
    scratch += [pltpu.VMEM((len(B_PATTERNS), SPAN, LANES), F32)] * 3
    args = []
    for q, k, v in qkv:
        args += [q, k, v, k, v]
    return pl.pallas_call(
        _attn_kernel,
        grid=(bsz, spans, N_PAIRS),
        in_specs=in_specs,
        out_specs=span_spec,
        out_shape=jax.ShapeDtypeStruct((N_PAIRS, t, LANES), BF16),
        scratch_shapes=scratch,
        compiler_params=_params("parallel", "parallel", "parallel"),
        name="attn",
    )(*args, bias, bg)


def _band_bias():
    qi = lax.broadcasted_iota(jnp.int32, (ATTN_BLOCK, 2 * ATTN_BLOCK), 0)
    kj = lax.broadcasted_iota(jnp.int32, (ATTN_BLOCK, 2 * ATTN_BLOCK), 1)
    dist = qi + ATTN_BLOCK - kj
    band = (dist >= 0) & (dist <= ATTN_BLOCK)
    first = band & (kj >= ATTN_BLOCK)
    return jnp.stack([jnp.where(first, 0.0, NEG), jnp.where(band, 0.0, NEG)]).astype(F32)


def _out_kernel(yb_ref, gb_ref, za_ref, x_ref, wob_ref, wout_ref, out_ref):
    yb = jnp.concatenate([yb_ref[hp] for hp in range(N_PAIRS)], axis=1)
    yb = jnp.dot(yb, wob_ref[...], preferred_element_type=F32)
    merged = za_ref[...].astype(F32) + gb_ref[...].astype(F32) * yb
    out_ref[...] = x_ref[...] + jnp.dot(merged.astype(BF16), wout_ref[...], preferred_element_type=F32)


def _out(yb, gb, za, x2, wob, wout, *, tm=512):
    t, d = x2.shape
    row = pl.BlockSpec((tm, d), lambda i: (i, 0))
    return pl.pallas_call(
        _out_kernel,
        grid=(t // tm,),
        in_specs=[pl.BlockSpec((N_PAIRS, tm, LANES), lambda i: (0, i, 0)), row, row, row,
                  _resident(wob.shape), _resident(wout.shape)],
        out_specs=row,
        out_shape=jax.ShapeDtypeStruct((t, d), F32),
        compiler_params=_params("parallel"),
        name="out",
    )(yb, gb, za, x2, wob, wout)


def kernel(x, norm_g, w_in, a_ws, a_bs, a_ln_g, a_ln_b, b_qn_g, b_kn_g, w_oa, w_ob, w_out):
    bsz, seq, d = x.shape
    depth = w_in.shape[0]
    a_width = w_oa.shape[1]
    npat = len(B_PATTERNS)
    assert w_in.shape[2] == 3 * a_width + 3 * npat * B_WIDTH + B_WIDTH + 2 * d
    assert all(w // dil == ATTN_BLOCK and SPAN % (ATTN_BLOCK * dil) == 0 for w, dil in B_PATTERNS)
    assert seq % SPAN == 0
    t = bsz * seq
    band_bias = _band_bias()
    x2 = x.reshape(t, d)
    for l in range(depth):
        sgu_bias = jnp.repeat(a_bs[l].T, a_width // A_GROUPS, axis=1)
        gq = jnp.tile(b_qn_g[l], (1, B_HEADS)).reshape(npat, 1, B_WIDTH)
        gk = jnp.tile(b_kn_g[l], (1, B_HEADS)).reshape(npat, 1, B_WIDTH)
        res = _fused_in(x2, norm_g[l].reshape(1, d), w_in[l].astype(BF16), a_ws[l], sgu_bias,
                        a_ln_g[l].reshape(1, -1), a_ln_b[l].reshape(1, -1), w_oa[l].astype(BF16),
                        gq, gk, bsz=bsz, seq=seq)
        za, gb, bg = res[:3]
        qkv = [res[3 + 3 * p:6 + 3 * p] for p in range(npat)]
        yb = _attn(qkv, band_bias, bg, bsz=bsz, seq=seq)
        x2 = _out(yb, gb, za, x2, w_ob[l].astype(BF16), w_out[l].astype(BF16))
    return x2.reshape(bsz, seq, d)
```

```python
import jax
import jax.numpy as jnp
from jax import lax
from jax.experimental import pallas as pl
from jax.experimental.pallas import tpu as pltpu

F32 = jnp.float32
BF16 = jnp.bfloat16

EPS = 1e-6
NEG = -1e30
CHUNK = 128
A_GROUPS = 4
B_PATTERNS = ((128, 1), (512, 4), (2048, 16))
B_HEADS = 8
B_HEAD_DIM = 64
B_WIDTH = B_HEADS * B_HEAD_DIM
LANES = 128
N_PAIRS = B_WIDTH // LANES
ATTN_BLOCK = 128
SPAN = 2048
UNITS = SPAN // ATTN_BLOCK

VMEM_LIMIT_BYTES = 56 * 1024 * 1024


def _params(*semantics):
    return pltpu.CompilerParams(dimension_semantics=semantics, vmem_limit_bytes=VMEM_LIMIT_BYTES)


def _resident(shape):
    return pl.BlockSpec(shape, lambda *_: (0,) * len(shape), pipeline_mode=pl.Buffered(1))


def _head_rms(t, gain):
    low = lax.broadcasted_iota(jnp.int32, (1, LANES), 1) < B_HEAD_DIM
    cols = []
    for cb in range(N_PAIRS):
        blk = t[:, cb * LANES:(cb + 1) * LANES]
        sq = blk * blk
        s0 = jnp.sum(jnp.where(low, sq, 0.0), axis=-1, keepdims=True)
        s1 = jnp.sum(jnp.where(low, 0.0, sq), axis=-1, keepdims=True)
        ms = jnp.where(low, s0, s1) * (1.0 / B_HEAD_DIM)
        cols.append(blk * lax.rsqrt(ms + EPS) * gain[:, cb * LANES:(cb + 1) * LANES])
    return jnp.concatenate(cols, axis=1)


def _emit_residue_major(val, out_ref):
    dilation, rows = out_ref.shape[1], out_ref.shape[2]
    for hp in range(N_PAIRS):
        for r in range(dilation):
            out_ref[hp, r] = val[r * rows:(r + 1) * rows, hp * LANES:(hp + 1) * LANES].astype(out_ref.dtype)


def _fused_in_kernel(x_ref, ng_ref, win_ref, ws_ref, sb_ref, lng_ref, lnb_ref, woa_ref, gq_ref, gk_ref,
                     za_ref, gb_ref, bg_ref, q0_ref, k0_ref, v0_ref, q1_ref, k1_ref, v1_ref,
                     q2_ref, k2_ref, v2_ref, h_ref, h4_ref, h16_ref, xs_ref, s_ref, vb_ref, yp_ref):
    tm, d_model = x_ref.shape
    a_width = woa_ref.shape[0]
    gw = a_width // A_GROUPS
    cbw = 512
    col_u, col_v, col_g = 0, a_width, 2 * a_width
    col_qkv = 3 * a_width
    col_bg = col_qkv + 3 * len(B_PATTERNS) * B_WIDTH
    col_ga = col_bg + B_WIDTH
    col_gb = col_ga + d_model

    x = x_ref[...]
    ms = jnp.mean(x * x, axis=-1, keepdims=True)
    xn = x * lax.rsqrt(ms + EPS) * ng_ref[...]
    h_ref[...] = xn.astype(BF16)
    for s in range(d_model // LANES):
        xs_ref[s] = xn[:, s * LANES:(s + 1) * LANES]
    h_by_dilation = {1: h_ref, 4: h4_ref, 16: h16_ref}
    for dil, hp_ref in ((4, h4_ref), (16, h16_ref)):
        rows = tm // dil
        for s in range(d_model // LANES):
            for r in range(dil):
                hp_ref[r * rows:(r + 1) * rows, s * LANES:(s + 1) * LANES] = (
                    xs_ref[s, pl.ds(r, rows, stride=dil), :].astype(BF16))

    def proj(col, width, src_ref=h_ref):
        return jnp.dot(src_ref[...], win_ref[:, col:col + width], preferred_element_type=F32)

    outs = ((q0_ref, k0_ref, v0_ref), (q1_ref, k1_ref, v1_ref), (q2_ref, k2_ref, v2_ref))

    def emit_qkv(p):
        q_ref, k_ref, v_ref = outs[p]
        hsrc = h_by_dilation[B_PATTERNS[p][1]]
        base = col_qkv + p * 3 * B_WIDTH
        _emit_residue_major(_head_rms(proj(base, B_WIDTH, hsrc), gq_ref[p] * (B_HEAD_DIM ** -0.5)), q_ref)
        _emit_residue_major(_head_rms(proj(base + B_WIDTH, B_WIDTH, hsrc), gk_ref[p]), k_ref)
        _emit_residue_major(proj(base + 2 * B_WIDTH, B_WIDTH, hsrc), v_ref)

    for cb in range(a_width // cbw):
        s_ref[:, cb * cbw:(cb + 1) * cbw] = jax.nn.gelu(proj(col_v + cb * cbw, cbw))
    emit_qkv(0)
    v = s_ref[...]
    mu = jnp.mean(v, axis=-1, keepdims=True)
    vc = v - mu
    var = jnp.mean(vc * vc, axis=-1, keepdims=True)
    vb_ref[...] = (vc * lax.rsqrt(var + EPS) * lng_ref[...] + lnb_ref[...]).astype(BF16)
    emit_qkv(1)
    row = lax.broadcasted_iota(jnp.int32, (CHUNK, CHUNK), 0)
    col = lax.broadcasted_iota(jnp.int32, (CHUNK, CHUNK), 1)
    for g in range(A_GROUPS):
        w = jnp.where(row >= col, ws_ref[g], 0.0).astype(BF16)
        gs = slice(g * gw, (g + 1) * gw)
        for c in range(tm // CHUNK):
            rs = slice(c * CHUNK, (c + 1) * CHUNK)
            s_ref[rs, gs] = jnp.dot(w, vb_ref[rs, gs], preferred_element_type=F32) + sb_ref[:, gs]
    emit_qkv(2)
    bg = jax.nn.silu(proj(col_bg, B_WIDTH))
    for hp in range(N_PAIRS):
        bg_ref[hp] = bg[:, hp * LANES:(hp + 1) * LANES].astype(bg_ref.dtype)
    for cb in range(a_width // cbw):
        cs = slice(cb * cbw, (cb + 1) * cbw)
        u = jax.nn.gelu(proj(col_u + cb * cbw, cbw))
        gate = jax.nn.silu(proj(col_g + cb * cbw, cbw))
        yp_ref[:, cs] = (u * s_ref[:, cs] * gate).astype(BF16)
    for cb in range(d_model // cbw):
        cs = slice(cb * cbw, (cb + 1) * cbw)
        gb_ref[:, cs] = jax.nn.sigmoid(proj(col_gb + cb * cbw, cbw)).astype(gb_ref.dtype)
        ya = jnp.dot(yp_ref[...], woa_ref[:, cs], preferred_element_type=F32)
        za_ref[:, cs] = (jax.nn.sigmoid(proj(col_ga + cb * cbw, cbw)) * ya).astype(za_ref.dtype)


def _fused_in(x2, ng, win, ws, sb, lng, lnb, woa, gq, gk, *, bsz, seq, tm=512):
    t, d = x2.shape
    a_width = woa.shape[0]
    tiles_per_seq = seq // tm
    row = lambda width: pl.BlockSpec((tm, width), lambda i: (i, 0))
    out_specs = [row(d), row(d), pl.BlockSpec((N_PAIRS, tm, LANES), lambda i: (0, i, 0))]
    out_shape = [jax.ShapeDtypeStruct((t, d), BF16), jax.ShapeDtypeStruct((t, d), BF16),
                 jax.ShapeDtypeStruct((N_PAIRS, t, LANES), BF16)]
    for _, dil in B_PATTERNS:
        spec = pl.BlockSpec((None, N_PAIRS, dil, tm // dil, LANES),
                            lambda i: (i // tiles_per_seq, 0, 0, i % tiles_per_seq, 0))
        shape = jax.ShapeDtypeStruct((bsz, N_PAIRS, dil, seq // dil, LANES), BF16)
        out_specs += [spec] * 3
        out_shape += [shape] * 3
    return pl.pallas_call(
        _fused_in_kernel,
        grid=(t // tm,),
        in_specs=[row(d), _resident(ng.shape), _resident(win.shape), _resident(ws.shape),
                  _resident(sb.shape), _resident(lng.shape), _resident(lnb.shape), _resident(woa.shape),
                  _resident(gq.shape), _resident(gk.shape)],
        out_specs=out_specs,
        out_shape=out_shape,
        scratch_shapes=[pltpu.VMEM((tm, d), BF16), pltpu.VMEM((tm, d), BF16), pltpu.VMEM((tm, d), BF16),
                        pltpu.VMEM((d // LANES, tm, LANES), F32), pltpu.VMEM((tm, a_width), F32),
                        pltpu.VMEM((tm, a_width), BF16), pltpu.VMEM((tm, a_width), BF16)],
        compiler_params=_params("parallel"),
        name="fused_in",
    )(x2, ng, win, ws, sb, lng, lnb, woa, gq, gk)


def _attn_unit(q2, k2, v2, bias, low):
    v2e = jnp.concatenate([v2, jnp.ones_like(v2)], axis=1)
    res = []
    for a in range(2):
        keep = low if a == 0 else jnp.logical_not(low)
        qa = jnp.where(keep, q2, jnp.zeros_like(q2))
        s = lax.dot_general(qa, k2, (((1,), (1,)), ((), ())), preferred_element_type=F32) + bias
        m = jnp.max(s, axis=-1, keepdims=True)
        e = jnp.exp(s - m).astype(BF16)
        oe = jnp.dot(e, v2e, preferred_element_type=F32)
        res.append((oe[:, :LANES], oe[:, LANES:], jnp.broadcast_to(m, (ATTN_BLOCK, LANES))))
    return tuple(jnp.where(low, res[0][i], res[1][i]) for i in range(3))


def _attn_kernel(q0_ref, k0_ref, v0_ref, kp0_ref, vp0_ref, q1_ref, k1_ref, v1_ref, kp1_ref, vp1_ref,
                 q2_ref, k2_ref, v2_ref, kp2_ref, vp2_ref, bias_ref, bg_ref, o_ref,
                 num_ref, den_ref, max_ref):
    first_span = pl.program_id(1) == 0
    low = lax.broadcasted_iota(jnp.int32, (1, LANES), 1) < B_HEAD_DIM
    pats = ((q0_ref, k0_ref, v0_ref, kp0_ref, vp0_ref),
            (q1_ref, k1_ref, v1_ref, kp1_ref, vp1_ref),
            (q2_ref, k2_ref, v2_ref, kp2_ref, vp2_ref))
    for p, (_, dil) in enumerate(B_PATTERNS):
        q_ref, k_ref, v_ref, kp_ref, vp_ref = pats[p]
        for r in range(dil):
            for c in range(UNITS // dil):
                rows = slice(c * ATTN_BLOCK, (c + 1) * ATTN_BLOCK)
                if c == 0:
                    k2 = jnp.concatenate([kp_ref[r], k_ref[r, rows, :]], axis=0)
                    v2 = jnp.concatenate([vp_ref[r], v_ref[r, rows, :]], axis=0)
                    bias = bias_ref[jnp.where(first_span, 0, 1)]
                else:
                    both = slice((c - 1) * ATTN_BLOCK, (c + 1) * ATTN_BLOCK)
                    k2, v2 = k_ref[r, both, :], v_ref[r, both, :]
                    bias = bias_ref[1]
                num, den, mx = _attn_unit(q_ref[r, rows, :], k2, v2, bias, low)
                dst = pl.ds(c * (ATTN_BLOCK * dil) + r, ATTN_BLOCK, stride=dil)
                num_ref[p, dst, :] = num
                den_ref[p, dst, :] = den
                max_ref[p, dst, :] = mx

    rows = 256
    for i in range(SPAN // rows):
        rs = slice(i * rows, (i + 1) * rows)
        m0, m1, m2 = max_ref[0, rs], max_ref[1, rs], max_ref[2, rs]
        mm = jnp.maximum(jnp.maximum(m0, m1), m2)
        a0, a1, a2 = jnp.exp(m0 - mm), jnp.exp(m1 - mm), jnp.exp(m2 - mm)
        num = a0 * num_ref[0, rs] + a1 * num_ref[1, rs] + a2 * num_ref[2, rs]
        den = a0 * den_ref[0, rs] + a1 * den_ref[1, rs] + a2 * den_ref[2, rs]
        o_ref[rs, :] = (num / den * bg_ref[rs, :].astype(F32)).astype(o_ref.dtype)


def _attn(qkv, bias, bg, *, bsz, seq):
    spans = seq // SPAN
    t = bsz * seq
    in_specs, scratch = [], []
    for _, dil in B_PATTERNS:
        rows = SPAN // dil
        cur = pl.BlockSpec((None, None, dil, rows, LANES), lambda b, c, hp: (b, hp, 0, c, 0))
        prev = pl.BlockSpec((None, None, dil, ATTN_BLOCK, LANES),
                            lambda b, c, hp, rows=rows: (b, hp, 0, jnp.maximum(c * (rows // ATTN_BLOCK) - 1, 0), 0))
        in_specs += [cur, cur, cur, prev, prev]
    span_spec = pl.BlockSpec((None, SPAN, LANES), lambda b, c, hp: (hp, b * spans + c, 0))
    in_specs += [pl.BlockSpec(bias.shape, lambda b, c, hp: (0, 0, 0)), span_spec]
    scratch += [pltpu.VMEM((len(B_PATTERNS), SPAN, LANES), F32)] * 3
    args = []
    for q, k, v in qkv:
        args += [q, k, v, k, v]
    return pl.pallas_call(
        _attn_kernel,
        grid=(bsz, spans, N_PAIRS),
        in_specs=in_specs,
        out_specs=span_spec,
        out_shape=jax.ShapeDtypeStruct((N_PAIRS, t, LANES), BF16),
        scratch_shapes=scratch,
        compiler_params=_params("parallel", "parallel", "parallel"),
        name="attn",
    )(*args, bias, bg)


def _band_bias():
    qi = lax.broadcasted_iota(jnp.int32, (ATTN_BLOCK, 2 * ATTN_BLOCK), 0)
    kj = lax.broadcasted_iota(jnp.int32, (ATTN_BLOCK, 2 * ATTN_BLOCK), 1)
    dist = qi + ATTN_BLOCK - kj
    band = (dist >= 0) & (dist <= ATTN_BLOCK)
    first = band & (kj >= ATTN_BLOCK)
    return jnp.stack([jnp.where(first, 0.0, NEG), jnp.where(band, 0.0, NEG)]).astype(F32)


def _out_kernel(yb_ref, gb_ref, za_ref, x_ref, wob_ref, wout_ref, out_ref):
    yb = jnp.concatenate([yb_ref[hp] for hp in range(N_PAIRS)], axis=1)
    yb = jnp.dot(yb, wob_ref[...], preferred_element_type=F32)
    merged = za_ref[...].astype(F32) + gb_ref[...].astype(F32) * yb
    out_ref[...] = x_ref[...] + jnp.dot(merged.astype(BF16), wout_ref[...], preferred_element_type=F32)


def _out(yb, gb, za, x2, wob, wout, *, tm=512):
    t, d = x2.shape
    row = pl.BlockSpec((tm, d), lambda i: (i, 0))
    return pl.pallas_call(
        _out_kernel,
        grid=(t // tm,),
        in_specs=[pl.BlockSpec((N_PAIRS, tm, LANES), lambda i: (0, i, 0)), row, row, row,
                  _resident(wob.shape), _resident(wout.shape)],
        out_specs=row,
        out_shape=jax.ShapeDtypeStruct((t, d), F32),
        compiler_params=_params("parallel"),
        name="out",
    )(yb, gb, za, x2, wob, wout)


def kernel(x, norm_g, w_in, a_ws, a_bs, a_ln_g, a_ln_b, b_qn_g, b_kn_g, w_oa, w_ob, w_out):
    bsz, seq, d = x.shape
    depth = w_in.shape[0]
    a_width = w_oa.shape[1]
    npat = len(B_PATTERNS)
    assert w_in.shape[2] == 3 * a_width + 3 * npat * B_WIDTH + B_WIDTH + 2 * d
    assert all(w // dil == ATTN_BLOCK and SPAN % (ATTN_BLOCK * dil) == 0 for w, dil in B_PATTERNS)
    assert seq % SPAN == 0
    t = bsz * seq
    band_bias = _band_bias()
    x2 = x.reshape(t, d)
    for l in range(depth):
        sgu_bias = jnp.repeat(a_bs[l].T, a_width // A_GROUPS, axis=1)
        gq = jnp.tile(b_qn_g[l], (1, B_HEADS)).reshape(npat, 1, B_WIDTH)
        gk = jnp.tile(b_kn_g[l], (1, B_HEADS)).reshape(npat, 1, B_WIDTH)
        res = _fused_in(x2, norm_g[l].reshape(1, d), w_in[l].astype(BF16), a_ws[l], sgu_bias,
                        a_ln_g[l].reshape(1, -1), a_ln_b[l].reshape(1, -1), w_oa[l].astype(BF16),
                        gq, gk, bsz=bsz, seq=seq)
        za, gb, bg = res[:3]
        qkv = [res[3 + 3 * p:6 + 3 * p] for p in range(npat)]
        yb = _attn(qkv, band_bias, bg, bsz=bsz, seq=seq)
        x2 = _out(yb, gb, za, x2, w_ob[l].astype(BF16), w_out[l].astype(BF16))
    return x2.reshape(bsz, seq, d)
```

```python
import jax
import jax.numpy as jnp
from jax import lax
from jax.experimental import pallas as pl
from jax.experimental.pallas import tpu as pltpu

F32 = jnp.float32
BF16 = jnp.bfloat16

EPS = 1e-6
NEG = -1e30
CHUNK = 128
A_GROUPS = 4
B_PATTERNS = ((128, 1), (512, 4), (2048, 16))
B_HEADS = 8
B_HEAD_DIM = 64
B_WIDTH = B_HEADS * B_HEAD_DIM
LANES = 128
N_PAIRS = B_WIDTH // LANES
ATTN_BLOCK = 128
SPAN = 2048
UNITS = SPAN // ATTN_BLOCK
Q_SCALE = B_HEAD_DIM ** -0.5 * 1.4426950408889634

VMEM_LIMIT_BYTES = 56 * 1024 * 1024


def _params(*semantics):
    return pltpu.CompilerParams(dimension_semantics=semantics, vmem_limit_bytes=VMEM_LIMIT_BYTES)


def _resident(shape):
    return pl.BlockSpec(shape, lambda *_: (0,) * len(shape), pipeline_mode=pl.Buffered(1))


def _head_rms(t, gain):
    low = lax.broadcasted_iota(jnp.int32, (1, LANES), 1) < B_HEAD_DIM
    cols = []
    for cb in range(N_PAIRS):
        blk = t[:, cb * LANES:(cb + 1) * LANES]
        sq = blk * blk
        s0 = jnp.sum(jnp.where(low, sq, 0.0), axis=-1, keepdims=True)
        s1 = jnp.sum(jnp.where(low, 0.0, sq), axis=-1, keepdims=True)
        ms = jnp.where(low, s0, s1) * (1.0 / B_HEAD_DIM)
        cols.append(blk * lax.rsqrt(ms + EPS) * gain[:, cb * LANES:(cb + 1) * LANES])
    return jnp.concatenate(cols, axis=1)


def _emit_residue_major(val, out_ref):
    dilation, rows = out_ref.shape[1], out_ref.shape[2]
    for hp in range(N_PAIRS):
        for r in range(dilation):
            out_ref[hp, r] = val[r * rows:(r + 1) * rows, hp * LANES:(hp + 1) * LANES].astype(out_ref.dtype)


def _fused_in_kernel(x_ref, ng_ref, win_ref, ws_ref, sb_ref, lng_ref, lnb_ref, woa_ref, gq_ref, gk_ref,
                     za_ref, gb_ref, bg_ref, q0_ref, k0_ref, v0_ref, q1_ref, k1_ref, v1_ref,
                     q2_ref, k2_ref, v2_ref, h_ref, h4_ref, h16_ref, xs_ref, s_ref, vb_ref, yp_ref):
    tm, d_model = x_ref.shape
    a_width = woa_ref.shape[0]
    gw = a_width // A_GROUPS
    cbw = 512
    col_u, col_v, col_g = 0, a_width, 2 * a_width
    col_qkv = 3 * a_width
    col_bg = col_qkv + 3 * len(B_PATTERNS) * B_WIDTH
    col_ga = col_bg + B_WIDTH
    col_gb = col_ga + d_model

    x = x_ref[...]
    ms = jnp.mean(x * x, axis=-1, keepdims=True)
    xn = x * lax.rsqrt(ms + EPS) * ng_ref[...]
    h_ref[...] = xn.astype(BF16)
    for s in range(d_model // LANES):
        xs_ref[s] = xn[:, s * LANES:(s + 1) * LANES]
    h_by_dilation = {1: h_ref, 4: h4_ref, 16: h16_ref}
    for dil, hp_ref in ((4, h4_ref), (16, h16_ref)):
        rows = tm // dil
        for s in range(d_model // LANES):
            for r in range(dil):
                hp_ref[r * rows:(r + 1) * rows, s * LANES:(s + 1) * LANES] = (
                    xs_ref[s, pl.ds(r, rows, stride=dil), :].astype(BF16))

    def proj(col, width, src_ref=h_ref):
        return jnp.dot(src_ref[...], win_ref[:, col:col + width], preferred_element_type=F32)

    outs = ((q0_ref, k0_ref, v0_ref), (q1_ref, k1_ref, v1_ref), (q2_ref, k2_ref, v2_ref))

    def emit_qkv(p):
        q_ref, k_ref, v_ref = outs[p]
        hsrc = h_by_dilation[B_PATTERNS[p][1]]
        base = col_qkv + p * 3 * B_WIDTH
        _emit_residue_major(_head_rms(proj(base, B_WIDTH, hsrc), gq_ref[p] * Q_SCALE), q_ref)
        _emit_residue_major(_head_rms(proj(base + B_WIDTH, B_WIDTH, hsrc), gk_ref[p]), k_ref)
        _emit_residue_major(proj(base + 2 * B_WIDTH, B_WIDTH, hsrc), v_ref)

    for cb in range(a_width // cbw):
        s_ref[:, cb * cbw:(cb + 1) * cbw] = jax.nn.gelu(proj(col_v + cb * cbw, cbw))
    emit_qkv(0)
    v = s_ref[...]
    mu = jnp.mean(v, axis=-1, keepdims=True)
    vc = v - mu
    var = jnp.mean(vc * vc, axis=-1, keepdims=True)
    vb_ref[...] = (vc * lax.rsqrt(var + EPS) * lng_ref[...] + lnb_ref[...]).astype(BF16)
    emit_qkv(1)
    row = lax.broadcasted_iota(jnp.int32, (CHUNK, CHUNK), 0)
    col = lax.broadcasted_iota(jnp.int32, (CHUNK, CHUNK), 1)
    for g in range(A_GROUPS):
        w = jnp.where(row >= col, ws_ref[g], 0.0).astype(BF16)
        gs = slice(g * gw, (g + 1) * gw)
        for c in range(tm // CHUNK):
            rs = slice(c * CHUNK, (c + 1) * CHUNK)
            s_ref[rs, gs] = jnp.dot(w, vb_ref[rs, gs], preferred_element_type=F32) + sb_ref[:, gs]
    emit_qkv(2)
    bg = jax.nn.silu(proj(col_bg, B_WIDTH))
    for hp in range(N_PAIRS):
        bg_ref[hp] = bg[:, hp * LANES:(hp + 1) * LANES].astype(bg_ref.dtype)
    for cb in range(a_width // cbw):
        cs = slice(cb * cbw, (cb + 1) * cbw)
        u = jax.nn.gelu(proj(col_u + cb * cbw, cbw))
        gate = jax.nn.silu(proj(col_g + cb * cbw, cbw))
        yp_ref[:, cs] = (u * s_ref[:, cs] * gate).astype(BF16)
    for cb in range(d_model // cbw):
        cs = slice(cb * cbw, (cb + 1) * cbw)
        gb_ref[:, cs] = jax.nn.sigmoid(proj(col_gb + cb * cbw, cbw)).astype(gb_ref.dtype)
        ya = jnp.dot(yp_ref[...], woa_ref[:, cs], preferred_element_type=F32)
        za_ref[:, cs] = (jax.nn.sigmoid(proj(col_ga + cb * cbw, cbw)) * ya).astype(za_ref.dtype)


def _fused_in(x2, ng, win, ws, sb, lng, lnb, woa, gq, gk, *, bsz, seq, tm=512):
    t, d = x2.shape
    a_width = woa.shape[0]
    tiles_per_seq = seq // tm
    row = lambda width: pl.BlockSpec((tm, width), lambda i: (i, 0))
    out_specs = [row(d), row(d), pl.BlockSpec((N_PAIRS, tm, LANES), lambda i: (0, i, 0))]
    out_shape = [jax.ShapeDtypeStruct((t, d), BF16), jax.ShapeDtypeStruct((t, d), BF16),
                 jax.ShapeDtypeStruct((N_PAIRS, t, LANES), BF16)]
    for _, dil in B_PATTERNS:
        spec = pl.BlockSpec((None, N_PAIRS, dil, tm // dil, LANES),
                            lambda i: (i // tiles_per_seq, 0, 0, i % tiles_per_seq, 0))
        shape = jax.ShapeDtypeStruct((bsz, N_PAIRS, dil, seq // dil, LANES), BF16)
        out_specs += [spec] * 3
        out_shape += [shape] * 3
    return pl.pallas_call(
        _fused_in_kernel,
        grid=(t // tm,),
        in_specs=[row(d), _resident(ng.shape), _resident(win.shape), _resident(ws.shape),
                  _resident(sb.shape), _resident(lng.shape), _resident(lnb.shape), _resident(woa.shape),
                  _resident(gq.shape), _resident(gk.shape)],
        out_specs=out_specs,
        out_shape=out_shape,
        scratch_shapes=[pltpu.VMEM((tm, d), BF16), pltpu.VMEM((tm, d), BF16), pltpu.VMEM((tm, d), BF16),
                        pltpu.VMEM((d // LANES, tm, LANES), F32), pltpu.VMEM((tm, a_width), F32),
                        pltpu.VMEM((tm, a_width), BF16), pltpu.VMEM((tm, a_width), BF16)],
        compiler_params=_params("parallel"),
        name="fused_in",
    )(x2, ng, win, ws, sb, lng, lnb, woa, gq, gk)


def _attn_unit(q2, k2, v2, bias, low):
    n = ATTN_BLOCK
    v2e = jnp.concatenate([v2, jnp.ones_like(v2)], axis=1)
    zero = jnp.zeros_like(q2)
    qs = jnp.concatenate([jnp.where(low, q2, zero), jnp.where(low, zero, q2)], axis=0)
    s = lax.dot_general(qs, k2, (((1,), (1,)), ((), ())), preferred_element_type=F32)
    s = s + jnp.concatenate([bias, bias], axis=0)
    m = jnp.max(s, axis=-1, keepdims=True)
    e = jnp.exp2(s - m).astype(BF16)
    oe = jnp.dot(e, v2e, preferred_element_type=F32)
    mb = jnp.broadcast_to(m, (2 * n, LANES))
    return (jnp.where(low, oe[:n, :LANES], oe[n:, :LANES]),
            jnp.where(low, oe[:n, LANES:], oe[n:, LANES:]),
            jnp.where(low, mb[:n], mb[n:]))


def _attn_kernel(q0_ref, k0_ref, v0_ref, kp0_ref, vp0_ref, q1_ref, k1_ref, v1_ref, kp1_ref, vp1_ref,
                 q2_ref, k2_ref, v2_ref, kp2_ref, vp2_ref, bias_ref, bg_ref, o_ref,
                 num_ref, den_ref, max_ref):
    first_span = pl.program_id(1) == 0
    low = lax.broadcasted_iota(jnp.int32, (1, LANES), 1) < B_HEAD_DIM
    pats = ((q0_ref, k0_ref, v0_ref, kp0_ref, vp0_ref),
            (q1_ref, k1_ref, v1_ref, kp1_ref, vp1_ref),
            (q2_ref, k2_ref, v2_ref, kp2_ref, vp2_ref))
    for p, (_, dil) in enumerate(B_PATTERNS):
        q_ref, k_ref, v_ref, kp_ref, vp_ref = pats[p]
        for r in range(dil):
            for c in range(UNITS // dil):
                rows = slice(c * ATTN_BLOCK, (c + 1) * ATTN_BLOCK)
                if c == 0:
                    k2 = jnp.concatenate([kp_ref[r], k_ref[r, rows, :]], axis=0)
                    v2 = jnp.concatenate([vp_ref[r], v_ref[r, rows, :]], axis=0)
                    bias = bias_ref[jnp.where(first_span, 0, 1)]
                else:
                    both = slice((c - 1) * ATTN_BLOCK, (c + 1) * ATTN_BLOCK)
                    k2, v2 = k_ref[r, both, :], v_ref[r, both, :]
                    bias = bias_ref[1]
                num, den, mx = _attn_unit(q_ref[r, rows, :], k2, v2, bias, low)
                dst = pl.ds(c * (ATTN_BLOCK * dil) + r, ATTN_BLOCK, stride=dil)
                num_ref[p, dst, :] = num
                den_ref[p, dst, :] = den
                max_ref[p, dst, :] = mx

    rows = 256
    for i in range(SPAN // rows):
        rs = slice(i * rows, (i + 1) * rows)
        m0, m1, m2 = max_ref[0, rs], max_ref[1, rs], max_ref[2, rs]
        mm = jnp.maximum(jnp.maximum(m0, m1), m2)
        a0, a1, a2 = jnp.exp2(m0 - mm), jnp.exp2(m1 - mm), jnp.exp2(m2 - mm)
        num = a0 * num_ref[0, rs] + a1 * num_ref[1, rs] + a2 * num_ref[2, rs]
        den = a0 * den_ref[0, rs] + a1 * den_ref[1, rs] + a2 * den_ref[2, rs]
        o_ref[rs, :] = (num / den * bg_ref[rs, :].astype(F32)).astype(o_ref.dtype)


def _attn(qkv, bias, bg, *, bsz, seq):
    spans = seq // SPAN
    t = bsz * seq
    in_specs, scratch = [], []
    for _, dil in B_PATTERNS:
        rows = SPAN // dil
        cur = pl.BlockSpec((None, None, dil, rows, LANES), lambda b, c, hp: (b, hp, 0, c, 0))
        prev = pl.BlockSpec((None, None, dil, ATTN_BLOCK, LANES),
                            lambda b, c, hp, rows=rows: (b, hp, 0, jnp.maximum(c * (rows // ATTN_BLOCK) - 1, 0), 0))
        in_specs += [cur, cur, cur, prev, prev]
    span_spec = pl.BlockSpec((None, SPAN, LANES), lambda b, c, hp: (hp, b * spans + c, 0))
    in_specs += [pl.BlockSpec(bias.shape, lambda b, c, hp: (0, 0, 0)), span_spec]
    scratch += [pltpu.VMEM((len(B_PATTERNS), SPAN, LANES), F32)] * 3
    args = []
    for q, k, v in qkv:
        args += [q, k, v, k, v]
    return pl.pallas_call(
        _attn_kernel,
        grid=(bsz, spans, N_PAIRS),
        in_specs=in_specs,
        out_specs=span_spec,
        out_shape=jax.ShapeDtypeStruct((N_PAIRS, t, LANES), BF16),
        scratch_shapes=scratch,
        compiler_params=_params("parallel", "parallel", "parallel"),
        name="attn",
    )(*args, bias, bg)


def _band_bias():
    qi = lax.broadcasted_iota(jnp.int32, (ATTN_BLOCK, 2 * ATTN_BLOCK), 0)
    kj = lax.broadcasted_iota(jnp.int32, (ATTN_BLOCK, 2 * ATTN_BLOCK), 1)
    dist = qi + ATTN_BLOCK - kj
    band = (dist >= 0) & (dist <= ATTN_BLOCK)
    first = band & (kj >= ATTN_BLOCK)
    return jnp.stack([jnp.where(first, 0.0, NEG), jnp.where(band, 0.0, NEG)]).astype(F32)


def _out_kernel(yb_ref, gb_ref, za_ref, x_ref, wob_ref, wout_ref, out_ref):
    yb = jnp.concatenate([yb_ref[hp] for hp in range(N_PAIRS)], axis=1)
    yb = jnp.dot(yb, wob_ref[...], preferred_element_type=F32)
    merged = za_ref[...].astype(F32) + gb_ref[...].astype(F32) * yb
    out_ref[...] = x_ref[...] + jnp.dot(merged.astype(BF16), wout_ref[...], preferred_element_type=F32)


def _out(yb, gb, za, x2, wob, wout, *, tm=512):
    t, d = x2.shape
    row = pl.BlockSpec((tm, d), lambda i: (i, 0))
    return pl.pallas_call(
        _out_kernel,
        grid=(t // tm,),
        in_specs=[pl.BlockSpec((N_PAIRS, tm, LANES), lambda i: (0, i, 0)), row, row, row,
                  _resident(wob.shape), _resident(wout.shape)],
        out_specs=row,
        out_shape=jax.ShapeDtypeStruct((t, d), F32),
        compiler_params=_params("parallel"),
        name="out",
    )(yb, gb, za, x2, wob, wout)


def kernel(x, norm_g, w_in, a_ws, a_bs, a_ln_g, a_ln_b, b_qn_g, b_kn_g, w_oa, w_ob, w_out):
    bsz, seq, d = x.shape
    depth = w_in.shape[0]
    a_width = w_oa.shape[1]
    npat = len(B_PATTERNS)
    assert w_in.shape[2] == 3 * a_width + 3 * npat * B_WIDTH + B_WIDTH + 2 * d
    assert all(w // dil == ATTN_BLOCK and SPAN % (ATTN_BLOCK * dil) == 0 for w, dil in B_PATTERNS)
    assert seq % SPAN == 0
    t = bsz * seq
    band_bias = _band_bias()
    x2 = x.reshape(t, d)
    for l in range(depth):
        sgu_bias = jnp.repeat(a_bs[l].T, a_width // A_GROUPS, axis=1)
        gq = jnp.tile(b_qn_g[l], (1, B_HEADS)).reshape(npat, 1, B_WIDTH)
        gk = jnp.tile(b_kn_g[l], (1, B_HEADS)).reshape(npat, 1, B_WIDTH)
        res = _fused_in(x2, norm_g[l].reshape(1, d), w_in[l].astype(BF16), a_ws[l], sgu_bias,
                        a_ln_g[l].reshape(1, -1), a_ln_b[l].reshape(1, -1), w_oa[l].astype(BF16),
                        gq, gk, bsz=bsz, seq=seq)
        za, gb, bg = res[:3]
        qkv = [res[3 + 3 * p:6 + 3 * p] for p in range(npat)]
        yb = _attn(qkv, band_bias, bg, bsz=bsz, seq=seq)
        x2 = _out(yb, gb, za, x2, w_ob[l].astype(BF16), w_out[l].astype(BF16))
    return x2.reshape(bsz, seq, d)
```

```python
import functools

import jax
import jax.numpy as jnp
from jax import lax
from jax.experimental import pallas as pl
from jax.experimental.pallas import tpu as pltpu

F32 = jnp.float32
BF16 = jnp.bfloat16

EPS = 1e-6
NEG = -1e30
CHUNK = 128
A_GROUPS = 4
B_PATTERNS = ((128, 1), (512, 4), (2048, 16))
B_HEADS = 8
B_HEAD_DIM = 64
B_WIDTH = B_HEADS * B_HEAD_DIM
LANES = 128
N_PAIRS = B_WIDTH // LANES
ATTN_BLOCK = 128
SPAN = 2048
UNITS = SPAN // ATTN_BLOCK
Q_SCALE = B_HEAD_DIM ** -0.5 * 1.4426950408889634

VMEM_LIMIT_BYTES = 56 * 1024 * 1024


def _params(*semantics):
    return pltpu.CompilerParams(dimension_semantics=semantics, vmem_limit_bytes=VMEM_LIMIT_BYTES)


def _resident(shape):
    return pl.BlockSpec(shape, lambda *_: (0,) * len(shape), pipeline_mode=pl.Buffered(1))


def _head_rms(t, gain):
    low = lax.broadcasted_iota(jnp.int32, (1, LANES), 1) < B_HEAD_DIM
    cols = []
    for cb in range(N_PAIRS):
        blk = t[:, cb * LANES:(cb + 1) * LANES]
        sq = blk * blk
        s0 = jnp.sum(jnp.where(low, sq, 0.0), axis=-1, keepdims=True)
        s1 = jnp.sum(jnp.where(low, 0.0, sq), axis=-1, keepdims=True)
        ms = jnp.where(low, s0, s1) * (1.0 / B_HEAD_DIM)
        cols.append(blk * lax.rsqrt(ms + EPS) * gain[:, cb * LANES:(cb + 1) * LANES])
    return jnp.concatenate(cols, axis=1)


def _emit_residue_major(val, out_ref):
    dilation, rows = out_ref.shape[1], out_ref.shape[2]
    for hp in range(N_PAIRS):
        for r in range(dilation):
            out_ref[hp, r] = val[r * rows:(r + 1) * rows, hp * LANES:(hp + 1) * LANES].astype(out_ref.dtype)


def _fused_in_kernel(x_ref, ng_ref, win_ref, ws_ref, sb_ref, lng_ref, lnb_ref, woa_ref, gq_ref, gk_ref,
                     za_ref, gb_ref, bg_ref, q0_ref, k0_ref, v0_ref, q1_ref, k1_ref, v1_ref,
                     q2_ref, k2_ref, v2_ref, h_ref, h4_ref, h16_ref, xs_ref, s_ref, vb_ref, yp_ref):
    tm, d_model = x_ref.shape
    a_width = woa_ref.shape[0]
    gw = a_width // A_GROUPS
    cbw = 512
    col_u, col_v, col_g = 0, a_width, 2 * a_width
    col_qkv = 3 * a_width
    col_bg = col_qkv + 3 * len(B_PATTERNS) * B_WIDTH
    col_ga = col_bg + B_WIDTH
    col_gb = col_ga + d_model

    x = x_ref[...]
    ms = jnp.mean(x * x, axis=-1, keepdims=True)
    xn = x * lax.rsqrt(ms + EPS) * ng_ref[...]
    h_ref[...] = xn.astype(BF16)
    for s in range(d_model // LANES):
        xs_ref[s] = xn[:, s * LANES:(s + 1) * LANES]
    h_by_dilation = {1: h_ref, 4: h4_ref, 16: h16_ref}
    for dil, hp_ref in ((4, h4_ref), (16, h16_ref)):
        rows = tm // dil
        for s in range(d_model // LANES):
            for r in range(dil):
                hp_ref[r * rows:(r + 1) * rows, s * LANES:(s + 1) * LANES] = (
                    xs_ref[s, pl.ds(r, rows, stride=dil), :].astype(BF16))

    def proj(col, width, src_ref=h_ref):
        return jnp.dot(src_ref[...], win_ref[:, col:col + width], preferred_element_type=F32)

    outs = ((q0_ref, k0_ref, v0_ref), (q1_ref, k1_ref, v1_ref), (q2_ref, k2_ref, v2_ref))

    def emit_qkv(p):
        q_ref, k_ref, v_ref = outs[p]
        hsrc = h_by_dilation[B_PATTERNS[p][1]]
        base = col_qkv + p * 3 * B_WIDTH
        _emit_residue_major(_head_rms(proj(base, B_WIDTH, hsrc), gq_ref[p] * Q_SCALE), q_ref)
        _emit_residue_major(_head_rms(proj(base + B_WIDTH, B_WIDTH, hsrc), gk_ref[p]), k_ref)
        _emit_residue_major(proj(base + 2 * B_WIDTH, B_WIDTH, hsrc), v_ref)

    for cb in range(a_width // cbw):
        s_ref[:, cb * cbw:(cb + 1) * cbw] = jax.nn.gelu(proj(col_v + cb * cbw, cbw))
    emit_qkv(0)
    v = s_ref[...]
    mu = jnp.mean(v, axis=-1, keepdims=True)
    vc = v - mu
    var = jnp.mean(vc * vc, axis=-1, keepdims=True)
    vb_ref[...] = (vc * lax.rsqrt(var + EPS) * lng_ref[...] + lnb_ref[...]).astype(BF16)
    emit_qkv(1)
    row = lax.broadcasted_iota(jnp.int32, (CHUNK, CHUNK), 0)
    col = lax.broadcasted_iota(jnp.int32, (CHUNK, CHUNK), 1)
    for g in range(A_GROUPS):
        w = jnp.where(row >= col, ws_ref[g], 0.0).astype(BF16)
        gs = slice(g * gw, (g + 1) * gw)
        for c in range(tm // CHUNK):
            rs = slice(c * CHUNK, (c + 1) * CHUNK)
            s_ref[rs, gs] = jnp.dot(w, vb_ref[rs, gs], preferred_element_type=F32) + sb_ref[:, gs]
    emit_qkv(2)
    bg = jax.nn.silu(proj(col_bg, B_WIDTH))
    for hp in range(N_PAIRS):
        bg_ref[hp] = bg[:, hp * LANES:(hp + 1) * LANES].astype(bg_ref.dtype)
    for cb in range(a_width // cbw):
        cs = slice(cb * cbw, (cb + 1) * cbw)
        u = jax.nn.gelu(proj(col_u + cb * cbw, cbw))
        gate = jax.nn.silu(proj(col_g + cb * cbw, cbw))
        yp_ref[:, cs] = (u * s_ref[:, cs] * gate).astype(BF16)
    for cb in range(d_model // cbw):
        cs = slice(cb * cbw, (cb + 1) * cbw)
        gb_ref[:, cs] = jax.nn.sigmoid(proj(col_gb + cb * cbw, cbw)).astype(gb_ref.dtype)
        ya = jnp.dot(yp_ref[...], woa_ref[:, cs], preferred_element_type=F32)
        za_ref[:, cs] = (jax.nn.sigmoid(proj(col_ga + cb * cbw, cbw)) * ya).astype(za_ref.dtype)


def _fused_in(x2, ng, win, ws, sb, lng, lnb, woa, gq, gk, *, bsz, seq, tm=512):
    t, d = x2.shape
    a_width = woa.shape[0]
    tiles_per_seq = seq // tm
    row = lambda width: pl.BlockSpec((tm, width), lambda i: (i, 0))
    out_specs = [row(d), row(d), pl.BlockSpec((N_PAIRS, tm, LANES), lambda i: (0, i, 0))]
    out_shape = [jax.ShapeDtypeStruct((t, d), BF16), jax.ShapeDtypeStruct((t, d), BF16),
                 jax.ShapeDtypeStruct((N_PAIRS, t, LANES), BF16)]
    for _, dil in B_PATTERNS:
        spec = pl.BlockSpec((None, N_PAIRS, dil, tm // dil, LANES),
                            lambda i: (i // tiles_per_seq, 0, 0, i % tiles_per_seq, 0))
        shape = jax.ShapeDtypeStruct((bsz, N_PAIRS, dil, seq // dil, LANES), BF16)
        out_specs += [spec] * 3
        out_shape += [shape] * 3
    return pl.pallas_call(
        _fused_in_kernel,
        grid=(t // tm,),
        in_specs=[row(d), _resident(ng.shape), _resident(win.shape), _resident(ws.shape),
                  _resident(sb.shape), _resident(lng.shape), _resident(lnb.shape), _resident(woa.shape),
                  _resident(gq.shape), _resident(gk.shape)],
        out_specs=out_specs,
        out_shape=out_shape,
        scratch_shapes=[pltpu.VMEM((tm, d), BF16), pltpu.VMEM((tm, d), BF16), pltpu.VMEM((tm, d), BF16),
                        pltpu.VMEM((d // LANES, tm, LANES), F32), pltpu.VMEM((tm, a_width), F32),
                        pltpu.VMEM((tm, a_width), BF16), pltpu.VMEM((tm, a_width), BF16)],
        compiler_params=_params("parallel"),
        name="fused_in",
    )(x2, ng, win, ws, sb, lng, lnb, woa, gq, gk)


def _attn_unit(q2, k2, v2, bias, low):
    n = ATTN_BLOCK
    v2e = jnp.concatenate([v2, jnp.ones_like(v2)], axis=1)
    zero = jnp.zeros_like(q2)
    qs = jnp.concatenate([jnp.where(low, q2, zero), jnp.where(low, zero, q2)], axis=0)
    s = lax.dot_general(qs, k2, (((1,), (1,)), ((), ())), preferred_element_type=F32)
    s = s + jnp.concatenate([bias, bias], axis=0)
    m = jnp.max(s, axis=-1, keepdims=True)
    e = jnp.exp2(s - m).astype(BF16)
    oe = jnp.dot(e, v2e, preferred_element_type=F32)
    mb = jnp.broadcast_to(m, (2 * n, LANES))
    return (jnp.where(low, oe[:n, :LANES], oe[n:, :LANES]),
            jnp.where(low, oe[:n, LANES:], oe[n:, LANES:]),
            jnp.where(low, mb[:n], mb[n:]))


def _attn_out_kernel(q0_ref, k0_ref, v0_ref, kp0_ref, vp0_ref, q1_ref, k1_ref, v1_ref, kp1_ref, vp1_ref,
                     q2_ref, k2_ref, v2_ref, kp2_ref, vp2_ref, bias_ref, bg_ref,
                     gb_ref, za_ref, x_ref, wob_ref, wout_ref, out_ref,
                     num_ref, den_ref, max_ref, yb_ref, *, n_spans, spans_per_seq):
    s = pl.program_id(0)
    hp = pl.program_id(1)
    tq = out_ref.shape[0]

    @pl.when(jnp.logical_and(s == 0, hp == 0))
    def _():
        yb_ref[...] = jnp.zeros_like(yb_ref)

    def out_projection(lo, n):
        src_rows = pl.ds(pl.multiple_of(hp * tq + lo, n), n)
        yb = jnp.concatenate([yb_ref[(s + 1) % 2, j, src_rows, :] for j in range(N_PAIRS)], axis=1)
        yb = jnp.dot(yb, wob_ref[...], preferred_element_type=F32)
        rs = slice(lo, lo + n)
        merged = za_ref[rs, :].astype(F32) + gb_ref[rs, :].astype(F32) * yb
        out_ref[rs, :] = x_ref[rs, :] + jnp.dot(merged.astype(BF16), wout_ref[...],
                                                 preferred_element_type=F32)

    @pl.when(s == n_spans)
    def _():
        out_projection(0, tq)

    @pl.when(s < n_spans)
    def _():
        first_span = s % spans_per_seq == 0
        low = lax.broadcasted_iota(jnp.int32, (1, LANES), 1) < B_HEAD_DIM
        pats = ((q0_ref, k0_ref, v0_ref, kp0_ref, vp0_ref),
                (q1_ref, k1_ref, v1_ref, kp1_ref, vp1_ref),
                (q2_ref, k2_ref, v2_ref, kp2_ref, vp2_ref))
        def unit(p, r, c):
            q_ref, k_ref, v_ref, kp_ref, vp_ref = pats[p]
            dil = B_PATTERNS[p][1]
            rows = slice(c * ATTN_BLOCK, (c + 1) * ATTN_BLOCK)
            if c == 0:
                k2 = jnp.concatenate([kp_ref[r], k_ref[r, rows, :]], axis=0)
                v2 = jnp.concatenate([vp_ref[r], v_ref[r, rows, :]], axis=0)
                bias = bias_ref[jnp.where(first_span, 0, 1)]
            else:
                both = slice((c - 1) * ATTN_BLOCK, (c + 1) * ATTN_BLOCK)
                k2, v2 = k_ref[r, both, :], v_ref[r, both, :]
                bias = bias_ref[1]
            num, den, mx = _attn_unit(q_ref[r, rows, :], k2, v2, bias, low)
            dst = pl.ds(c * (ATTN_BLOCK * dil) + r, ATTN_BLOCK, stride=dil)
            num_ref[p, dst, :] = num
            den_ref[p, dst, :] = den
            max_ref[p, dst, :] = mx

        def combine(rs):
            m0, m1, m2 = max_ref[0, rs], max_ref[1, rs], max_ref[2, rs]
            mm = jnp.maximum(jnp.maximum(m0, m1), m2)
            a0, a1, a2 = jnp.exp2(m0 - mm), jnp.exp2(m1 - mm), jnp.exp2(m2 - mm)
            num = a0 * num_ref[0, rs] + a1 * num_ref[1, rs] + a2 * num_ref[2, rs]
            den = a0 * den_ref[0, rs] + a1 * den_ref[1, rs] + a2 * den_ref[2, rs]
            yb_ref[s % 2, hp, rs, :] = (num / den * bg_ref[rs, :].astype(F32)).astype(yb_ref.dtype)

        p16, p4, p1 = 2, 1, 0
        assert [B_PATTERNS[p][1] for p in (p16, p4, p1)] == [16, 4, 1]
        half = tq // 2
        for part in range(2):
            out_projection(part * half, half)
            for r in range(part * 8, part * 8 + 8):
                unit(p16, r, 0)
        for c4 in range(UNITS // 4):
            for r in range(4):
                unit(p4, r, c4)
            for c in range(4 * c4, 4 * c4 + 4):
                unit(p1, 0, c)
                if c % 2 == 1:
                    combine(slice((c - 1) * ATTN_BLOCK, (c + 1) * ATTN_BLOCK))


def _attn_out(qkv, bias, bg, gb, za, x2, wob, wout, *, bsz, seq):
    spans = seq // SPAN
    n_spans = bsz * spans
    t, d = x2.shape
    tq = SPAN // N_PAIRS

    def span_of(s):
        sc = jnp.minimum(s, n_spans - 1)
        return sc // spans, sc % spans

    in_specs = []
    for _, dil in B_PATTERNS:
        rows = SPAN // dil

        def cur_map(s, hp):
            b, c = span_of(s)
            return (b, hp, 0, c, 0)

        def prev_map(s, hp, rows=rows):
            b, c = span_of(s)
            return (b, hp, 0, jnp.maximum(c * (rows // ATTN_BLOCK) - 1, 0), 0)

        cur = pl.BlockSpec((None, None, dil, rows, LANES), cur_map)
        prev = pl.BlockSpec((None, None, dil, ATTN_BLOCK, LANES), prev_map)
        in_specs += [cur, cur, cur, prev, prev]
    tail = pl.BlockSpec((tq, d), lambda s, hp: (jnp.maximum((s - 1) * N_PAIRS + hp, 0), 0))
    in_specs += [pl.BlockSpec(bias.shape, lambda s, hp: (0, 0, 0)),
                 pl.BlockSpec((None, SPAN, LANES), lambda s, hp: (hp, jnp.minimum(s, n_spans - 1), 0)),
                 tail, tail, tail, _resident(wob.shape), _resident(wout.shape)]
    args = []
    for q, k, v in qkv:
        args += [q, k, v, k, v]
    return pl.pallas_call(
        functools.partial(_attn_out_kernel, n_spans=n_spans, spans_per_seq=spans),
        grid=(n_spans + 1, N_PAIRS),
        in_specs=in_specs,
        out_specs=tail,
        out_shape=jax.ShapeDtypeStruct((t, d), F32),
        scratch_shapes=[pltpu.VMEM((len(B_PATTERNS), SPAN, LANES), F32)] * 3
        + [pltpu.VMEM((2, N_PAIRS, SPAN, LANES), BF16)],
        compiler_params=_params("arbitrary", "arbitrary"),
        name="attn_out",
    )(*args, bias, bg, gb, za, x2, wob, wout)


def _band_bias():
    qi = lax.broadcasted_iota(jnp.int32, (ATTN_BLOCK, 2 * ATTN_BLOCK), 0)
    kj = lax.broadcasted_iota(jnp.int32, (ATTN_BLOCK, 2 * ATTN_BLOCK), 1)
    dist = qi + ATTN_BLOCK - kj
    band = (dist >= 0) & (dist <= ATTN_BLOCK)
    first = band & (kj >= ATTN_BLOCK)
    return jnp.stack([jnp.where(first, 0.0, NEG), jnp.where(band, 0.0, NEG)]).astype(F32)


def kernel(x, norm_g, w_in, a_ws, a_bs, a_ln_g, a_ln_b, b_qn_g, b_kn_g, w_oa, w_ob, w_out):
    bsz, seq, d = x.shape
    depth = w_in.shape[0]
    a_width = w_oa.shape[1]
    npat = len(B_PATTERNS)
    assert w_in.shape[2] == 3 * a_width + 3 * npat * B_WIDTH + B_WIDTH + 2 * d
    assert all(w // dil == ATTN_BLOCK and SPAN % (ATTN_BLOCK * dil) == 0 for w, dil in B_PATTERNS)
    assert seq % SPAN == 0
    t = bsz * seq
    band_bias = _band_bias()
    x2 = x.reshape(t, d)
    for l in range(depth):
        sgu_bias = jnp.repeat(a_bs[l].T, a_width // A_GROUPS, axis=1)
        gq = jnp.tile(b_qn_g[l], (1, B_HEADS)).reshape(npat, 1, B_WIDTH)
        gk = jnp.tile(b_kn_g[l], (1, B_HEADS)).reshape(npat, 1, B_WIDTH)
        res = _fused_in(x2, norm_g[l].reshape(1, d), w_in[l].astype(BF16), a_ws[l], sgu_bias,
                        a_ln_g[l].reshape(1, -1), a_ln_b[l].reshape(1, -1), w_oa[l].astype(BF16),
                        gq, gk, bsz=bsz, seq=seq)
        za, gb, bg = res[:3]
        qkv = [res[3 + 3 * p:6 + 3 * p] for p in range(npat)]
        x2 = _attn_out(qkv, band_bias, bg, gb, za, x2, w_ob[l].astype(BF16), w_out[l].astype(BF16),
                       bsz=bsz, seq=seq)
    return x2.reshape(bsz, seq, d)
```

```python
import functools

import jax
import jax.numpy as jnp
from jax import lax
from jax.experimental import pallas as pl
from jax.experimental.pallas import tpu as pltpu

F32 = jnp.float32
BF16 = jnp.bfloat16

EPS = 1e-6
NEG = -1e30
CHUNK = 128
A_GROUPS = 4
B_PATTERNS = ((128, 1), (512, 4), (2048, 16))
B_HEADS = 8
B_HEAD_DIM = 64
B_WIDTH = B_HEADS * B_HEAD_DIM
LANES = 128
N_PAIRS = B_WIDTH // LANES
ATTN_BLOCK = 128
SPAN = 2048
UNITS = SPAN // ATTN_BLOCK
Q_SCALE = B_HEAD_DIM ** -0.5 * 1.4426950408889634

VMEM_LIMIT_BYTES = 60 * 1024 * 1024


def _params(*semantics):
    return pltpu.CompilerParams(dimension_semantics=semantics, vmem_limit_bytes=VMEM_LIMIT_BYTES)


def _resident(shape):
    return pl.BlockSpec(shape, lambda *_: (0,) * len(shape), pipeline_mode=pl.Buffered(1))


WEIGHT_CHUNK = 256


def _weight_scratch(max_rows):
    return [pltpu.VMEM((2, max_rows, WEIGHT_CHUNK), F32), pltpu.SemaphoreType.DMA((2,))]


def _load_weight_bf16(src_hbm, dst_ref, stage_ref, sem):
    rows, cols = src_hbm.shape
    n = cols // WEIGHT_CHUNK

    def copy(c):
        return pltpu.make_async_copy(src_hbm.at[:, pl.ds(c * WEIGHT_CHUNK, WEIGHT_CHUNK)],
                                     stage_ref.at[c % 2, pl.ds(0, rows)], sem.at[c % 2])

    copy(0).start()
    for c in range(n):
        if c + 1 < n:
            copy(c + 1).start()
        copy(c).wait()
        dst_ref[:, c * WEIGHT_CHUNK:(c + 1) * WEIGHT_CHUNK] = stage_ref[c % 2, :rows, :].astype(BF16)


def _head_rms(t, gain):
    low = lax.broadcasted_iota(jnp.int32, (1, LANES), 1) < B_HEAD_DIM
    cols = []
    for cb in range(N_PAIRS):
        blk = t[:, cb * LANES:(cb + 1) * LANES]
        sq = blk * blk
        s0 = jnp.sum(jnp.where(low, sq, 0.0), axis=-1, keepdims=True)
        s1 = jnp.sum(jnp.where(low, 0.0, sq), axis=-1, keepdims=True)
        ms = jnp.where(low, s0, s1) * (1.0 / B_HEAD_DIM)
        cols.append(blk * lax.rsqrt(ms + EPS) * gain[:, cb * LANES:(cb + 1) * LANES])
    return jnp.concatenate(cols, axis=1)


def _emit_residue_major(val, out_ref):
    dilation, rows = out_ref.shape[1], out_ref.shape[2]
    for hp in range(N_PAIRS):
        for r in range(dilation):
            out_ref[hp, r] = val[r * rows:(r + 1) * rows, hp * LANES:(hp + 1) * LANES].astype(out_ref.dtype)


def _fused_in_kernel(x_ref, ng_ref, win_ref, ws_ref, sb_ref, lng_ref, lnb_ref, woa_ref, gq_ref, gk_ref,
                     za_ref, gb_ref, bg_ref, q0_ref, k0_ref, v0_ref, q1_ref, k1_ref, v1_ref,
                     q2_ref, k2_ref, v2_ref, h_ref, h4_ref, h16_ref, xs_ref, s_ref, vb_ref, yp_ref,
                     win_bf_ref, woa_bf_ref, stage_ref, sem):
    tm, d_model = x_ref.shape
    a_width = woa_ref.shape[0]

    @pl.when(pl.program_id(0) == 0)
    def _():
        _load_weight_bf16(win_ref, win_bf_ref, stage_ref, sem)
        _load_weight_bf16(woa_ref, woa_bf_ref, stage_ref, sem)

    gw = a_width // A_GROUPS
    cbw = 512
    col_u, col_v, col_g = 0, a_width, 2 * a_width
    col_qkv = 3 * a_width
    col_bg = col_qkv + 3 * len(B_PATTERNS) * B_WIDTH
    col_ga = col_bg + B_WIDTH
    col_gb = col_ga + d_model

    x = x_ref[...]
    ms = jnp.mean(x * x, axis=-1, keepdims=True)
    xn = x * lax.rsqrt(ms + EPS) * ng_ref[...]
    h_ref[...] = xn.astype(BF16)
    for s in range(d_model // LANES):
        xs_ref[s] = xn[:, s * LANES:(s + 1) * LANES]
    h_by_dilation = {1: h_ref, 4: h4_ref, 16: h16_ref}
    for dil, hp_ref in ((4, h4_ref), (16, h16_ref)):
        rows = tm // dil
        for s in range(d_model // LANES):
            for r in range(dil):
                hp_ref[r * rows:(r + 1) * rows, s * LANES:(s + 1) * LANES] = (
                    xs_ref[s, pl.ds(r, rows, stride=dil), :].astype(BF16))

    def proj(col, width, src_ref=h_ref):
        return jnp.dot(src_ref[...], win_bf_ref[:, col:col + width], preferred_element_type=F32)

    outs = ((q0_ref, k0_ref, v0_ref), (q1_ref, k1_ref, v1_ref), (q2_ref, k2_ref, v2_ref))

    def emit_qkv(p):
        q_ref, k_ref, v_ref = outs[p]
        hsrc = h_by_dilation[B_PATTERNS[p][1]]
        base = col_qkv + p * 3 * B_WIDTH
        _emit_residue_major(_head_rms(proj(base, B_WIDTH, hsrc), gq_ref[p] * Q_SCALE), q_ref)
        _emit_residue_major(_head_rms(proj(base + B_WIDTH, B_WIDTH, hsrc), gk_ref[p]), k_ref)
        _emit_residue_major(proj(base + 2 * B_WIDTH, B_WIDTH, hsrc), v_ref)

    for cb in range(a_width // cbw):
        s_ref[:, cb * cbw:(cb + 1) * cbw] = jax.nn.gelu(proj(col_v + cb * cbw, cbw))
    emit_qkv(0)
    v = s_ref[...]
    mu = jnp.mean(v, axis=-1, keepdims=True)
    vc = v - mu
    var = jnp.mean(vc * vc, axis=-1, keepdims=True)
    vb_ref[...] = (vc * lax.rsqrt(var + EPS) * lng_ref[...] + lnb_ref[...]).astype(BF16)
    emit_qkv(1)
    row = lax.broadcasted_iota(jnp.int32, (CHUNK, CHUNK), 0)
    col = lax.broadcasted_iota(jnp.int32, (CHUNK, CHUNK), 1)
    for g in range(A_GROUPS):
        w = jnp.where(row >= col, ws_ref[g], 0.0).astype(BF16)
        gs = slice(g * gw, (g + 1) * gw)
        for c in range(tm // CHUNK):
            rs = slice(c * CHUNK, (c + 1) * CHUNK)
            s_ref[rs, gs] = jnp.dot(w, vb_ref[rs, gs], preferred_element_type=F32) + sb_ref[:, gs]
    emit_qkv(2)
    bg = jax.nn.silu(proj(col_bg, B_WIDTH))
    for hp in range(N_PAIRS):
        bg_ref[hp] = bg[:, hp * LANES:(hp + 1) * LANES].astype(bg_ref.dtype)
    for cb in range(a_width // cbw):
        cs = slice(cb * cbw, (cb + 1) * cbw)
        u = jax.nn.gelu(proj(col_u + cb * cbw, cbw))
        gate = jax.nn.silu(proj(col_g + cb * cbw, cbw))
        yp_ref[:, cs] = (u * s_ref[:, cs] * gate).astype(BF16)
    for cb in range(d_model // cbw):
        cs = slice(cb * cbw, (cb + 1) * cbw)
        gb_ref[:, cs] = jax.nn.sigmoid(proj(col_gb + cb * cbw, cbw)).astype(gb_ref.dtype)
        ya = jnp.dot(yp_ref[...], woa_bf_ref[:, cs], preferred_element_type=F32)
        za_ref[:, cs] = (jax.nn.sigmoid(proj(col_ga + cb * cbw, cbw)) * ya).astype(za_ref.dtype)


def _fused_in(x2, ng, win, ws, sb, lng, lnb, woa, gq, gk, *, bsz, seq, tm=512):
    t, d = x2.shape
    a_width = woa.shape[0]
    tiles_per_seq = seq // tm
    row = lambda width: pl.BlockSpec((tm, width), lambda i: (i, 0))
    hbm = pl.BlockSpec(memory_space=pl.ANY)
    out_specs = [row(d), row(d), pl.BlockSpec((N_PAIRS, tm, LANES), lambda i: (0, i, 0))]
    out_shape = [jax.ShapeDtypeStruct((t, d), BF16), jax.ShapeDtypeStruct((t, d), BF16),
                 jax.ShapeDtypeStruct((N_PAIRS, t, LANES), BF16)]
    for _, dil in B_PATTERNS:
        spec = pl.BlockSpec((None, N_PAIRS, dil, tm // dil, LANES),
                            lambda i: (i // tiles_per_seq, 0, 0, i % tiles_per_seq, 0))
        shape = jax.ShapeDtypeStruct((bsz, N_PAIRS, dil, seq // dil, LANES), BF16)
        out_specs += [spec] * 3
        out_shape += [shape] * 3
    return pl.pallas_call(
        _fused_in_kernel,
        grid=(t // tm,),
        in_specs=[row(d), _resident(ng.shape), hbm, _resident(ws.shape),
                  _resident(sb.shape), _resident(lng.shape), _resident(lnb.shape), hbm,
                  _resident(gq.shape), _resident(gk.shape)],
        out_specs=out_specs,
        out_shape=out_shape,
        scratch_shapes=[pltpu.VMEM((tm, d), BF16), pltpu.VMEM((tm, d), BF16), pltpu.VMEM((tm, d), BF16),
                        pltpu.VMEM((d // LANES, tm, LANES), F32), pltpu.VMEM((tm, a_width), F32),
                        pltpu.VMEM((tm, a_width), BF16), pltpu.VMEM((tm, a_width), BF16),
                        pltpu.VMEM(win.shape, BF16), pltpu.VMEM(woa.shape, BF16)]
        + _weight_scratch(max(win.shape[0], woa.shape[0])),
        compiler_params=_params("arbitrary"),
        name="fused_in",
    )(x2, ng, win, ws, sb, lng, lnb, woa, gq, gk)


def _attn_unit(q2, k2, v2, bias, low):
    n = ATTN_BLOCK
    v2e = jnp.concatenate([v2, jnp.ones_like(v2)], axis=1)
    zero = jnp.zeros_like(q2)
    qs = jnp.concatenate([jnp.where(low, q2, zero), jnp.where(low, zero, q2)], axis=0)
    s = lax.dot_general(qs, k2, (((1,), (1,)), ((), ())), preferred_element_type=F32)
    s = s + jnp.concatenate([bias, bias], axis=0)
    m = jnp.max(s, axis=-1, keepdims=True)
    e = jnp.exp2(s - m).astype(BF16)
    oe = jnp.dot(e, v2e, preferred_element_type=F32)
    mb = jnp.broadcast_to(m, (2 * n, LANES))
    return (jnp.where(low, oe[:n, :LANES], oe[n:, :LANES]),
            jnp.where(low, oe[:n, LANES:], oe[n:, LANES:]),
            jnp.where(low, mb[:n], mb[n:]))


def _attn_out_kernel(q0_ref, k0_ref, v0_ref, kp0_ref, vp0_ref, q1_ref, k1_ref, v1_ref, kp1_ref, vp1_ref,
                     q2_ref, k2_ref, v2_ref, kp2_ref, vp2_ref, bias_ref, bg_ref,
                     gb_ref, za_ref, x_ref, wob_ref, wout_ref, out_ref,
                     num_ref, den_ref, max_ref, yb_ref, wob_bf_ref, wout_bf_ref, stage_ref, sem,
                     *, n_spans, spans_per_seq):
    s = pl.program_id(0)
    hp = pl.program_id(1)
    tq = out_ref.shape[0]

    @pl.when(jnp.logical_and(s == 0, hp == 0))
    def _():
        yb_ref[...] = jnp.zeros_like(yb_ref)
        _load_weight_bf16(wob_ref, wob_bf_ref, stage_ref, sem)
        _load_weight_bf16(wout_ref, wout_bf_ref, stage_ref, sem)

    def out_projection(lo, n):
        src_rows = pl.ds(pl.multiple_of(hp * tq + lo, n), n)
        yb = jnp.concatenate([yb_ref[(s + 1) % 2, j, src_rows, :] for j in range(N_PAIRS)], axis=1)
        yb = jnp.dot(yb, wob_bf_ref[...], preferred_element_type=F32)
        rs = slice(lo, lo + n)
        merged = za_ref[rs, :].astype(F32) + gb_ref[rs, :].astype(F32) * yb
        out_ref[rs, :] = x_ref[rs, :] + jnp.dot(merged.astype(BF16), wout_bf_ref[...],
                                                 preferred_element_type=F32)

    @pl.when(s == n_spans)
    def _():
        out_projection(0, tq)

    @pl.when(s < n_spans)
    def _():
        first_span = s % spans_per_seq == 0
        low = lax.broadcasted_iota(jnp.int32, (1, LANES), 1) < B_HEAD_DIM
        pats = ((q0_ref, k0_ref, v0_ref, kp0_ref, vp0_ref),
                (q1_ref, k1_ref, v1_ref, kp1_ref, vp1_ref),
                (q2_ref, k2_ref, v2_ref, kp2_ref, vp2_ref))
        def unit(p, r, c):
            q_ref, k_ref, v_ref, kp_ref, vp_ref = pats[p]
            dil = B_PATTERNS[p][1]
            rows = slice(c * ATTN_BLOCK, (c + 1) * ATTN_BLOCK)
            if c == 0:
                k2 = jnp.concatenate([kp_ref[r], k_ref[r, rows, :]], axis=0)
                v2 = jnp.concatenate([vp_ref[r], v_ref[r, rows, :]], axis=0)
                bias = bias_ref[jnp.where(first_span, 0, 1)]
            else:
                both = slice((c - 1) * ATTN_BLOCK, (c + 1) * ATTN_BLOCK)
                k2, v2 = k_ref[r, both, :], v_ref[r, both, :]
                bias = bias_ref[1]
            num, den, mx = _attn_unit(q_ref[r, rows, :], k2, v2, bias, low)
            dst = pl.ds(c * (ATTN_BLOCK * dil) + r, ATTN_BLOCK, stride=dil)
            num_ref[p, dst, :] = num
            den_ref[p, dst, :] = den
            max_ref[p, dst, :] = mx

        def combine(rs):
            m0, m1, m2 = max_ref[0, rs], max_ref[1, rs], max_ref[2, rs]
            mm = jnp.maximum(jnp.maximum(m0, m1), m2)
            a0, a1, a2 = jnp.exp2(m0 - mm), jnp.exp2(m1 - mm), jnp.exp2(m2 - mm)
            num = a0 * num_ref[0, rs] + a1 * num_ref[1, rs] + a2 * num_ref[2, rs]
            den = a0 * den_ref[0, rs] + a1 * den_ref[1, rs] + a2 * den_ref[2, rs]
            yb_ref[s % 2, hp, rs, :] = (num / den * bg_ref[rs, :].astype(F32)).astype(yb_ref.dtype)

        p16, p4, p1 = 2, 1, 0
        assert [B_PATTERNS[p][1] for p in (p16, p4, p1)] == [16, 4, 1]
        parts = 2
        for part in range(parts):
            out_projection(part * (tq // parts), tq // parts)
            for r in range(part * (16 // parts), (part + 1) * (16 // parts)):
                unit(p16, r, 0)
        for c4 in range(UNITS // 4):
            for r in range(4):
                unit(p4, r, c4)
            for c in range(4 * c4, 4 * c4 + 4):
                unit(p1, 0, c)
                if c % 2 == 1:
                    combine(slice((c - 1) * ATTN_BLOCK, (c + 1) * ATTN_BLOCK))


def _attn_out(qkv, bias, bg, gb, za, x2, wob, wout, *, bsz, seq):
    spans = seq // SPAN
    n_spans = bsz * spans
    t, d = x2.shape
    tq = SPAN // N_PAIRS

    def span_of(s):
        sc = jnp.minimum(s, n_spans - 1)
        return sc // spans, sc % spans

    in_specs = []
    for _, dil in B_PATTERNS:
        rows = SPAN // dil

        def cur_map(s, hp):
            b, c = span_of(s)
            return (b, hp, 0, c, 0)

        def prev_map(s, hp, rows=rows):
            b, c = span_of(s)
            return (b, hp, 0, jnp.maximum(c * (rows // ATTN_BLOCK) - 1, 0), 0)

        cur = pl.BlockSpec((None, None, dil, rows, LANES), cur_map)
        prev = pl.BlockSpec((None, None, dil, ATTN_BLOCK, LANES), prev_map)
        in_specs += [cur, cur, cur, prev, prev]
    tail = pl.BlockSpec((tq, d), lambda s, hp: (jnp.maximum((s - 1) * N_PAIRS + hp, 0), 0))
    in_specs += [pl.BlockSpec(bias.shape, lambda s, hp: (0, 0, 0)),
                 pl.BlockSpec((None, SPAN, LANES), lambda s, hp: (hp, jnp.minimum(s, n_spans - 1), 0)),
                 tail, tail, tail, pl.BlockSpec(memory_space=pl.ANY), pl.BlockSpec(memory_space=pl.ANY)]
    args = []
    for q, k, v in qkv:
        args += [q, k, v, k, v]
    return pl.pallas_call(
        functools.partial(_attn_out_kernel, n_spans=n_spans, spans_per_seq=spans),
        grid=(n_spans + 1, N_PAIRS),
        in_specs=in_specs,
        out_specs=tail,
        out_shape=jax.ShapeDtypeStruct((t, d), F32),
        scratch_shapes=[pltpu.VMEM((len(B_PATTERNS), SPAN, LANES), F32)] * 3
        + [pltpu.VMEM((2, N_PAIRS, SPAN, LANES), BF16), pltpu.VMEM(wob.shape, BF16), pltpu.VMEM(wout.shape, BF16)]
        + _weight_scratch(max(wob.shape[0], wout.shape[0])),
        compiler_params=_params("arbitrary", "arbitrary"),
        name="attn_out",
    )(*args, bias, bg, gb, za, x2, wob, wout)


def _band_bias():
    qi = lax.broadcasted_iota(jnp.int32, (ATTN_BLOCK, 2 * ATTN_BLOCK), 0)
    kj = lax.broadcasted_iota(jnp.int32, (ATTN_BLOCK, 2 * ATTN_BLOCK), 1)
    dist = qi + ATTN_BLOCK - kj
    band = (dist >= 0) & (dist <= ATTN_BLOCK)
    first = band & (kj >= ATTN_BLOCK)
    return jnp.stack([jnp.where(first, 0.0, NEG), jnp.where(band, 0.0, NEG)]).astype(F32)


def kernel(x, norm_g, w_in, a_ws, a_bs, a_ln_g, a_ln_b, b_qn_g, b_kn_g, w_oa, w_ob, w_out):
    bsz, seq, d = x.shape
    depth = w_in.shape[0]
    a_width = w_oa.shape[1]
    npat = len(B_PATTERNS)
    assert w_in.shape[2] == 3 * a_width + 3 * npat * B_WIDTH + B_WIDTH + 2 * d
    assert all(w // dil == ATTN_BLOCK and SPAN % (ATTN_BLOCK * dil) == 0 for w, dil in B_PATTERNS)
    assert seq % SPAN == 0
    t = bsz * seq
    band_bias = _band_bias()
    x2 = x.reshape(t, d)
    for l in range(depth):
        sgu_bias = jnp.repeat(a_bs[l].T, a_width // A_GROUPS, axis=1)
        gq = jnp.tile(b_qn_g[l], (1, B_HEADS)).reshape(npat, 1, B_WIDTH)
        gk = jnp.tile(b_kn_g[l], (1, B_HEADS)).reshape(npat, 1, B_WIDTH)
        res = _fused_in(x2, norm_g[l].reshape(1, d), w_in[l], a_ws[l], sgu_bias,
                        a_ln_g[l].reshape(1, -1), a_ln_b[l].reshape(1, -1), w_oa[l],
                        gq, gk, bsz=bsz, seq=seq)
        za, gb, bg = res[:3]
        qkv = [res[3 + 3 * p:6 + 3 * p] for p in range(npat)]
        x2 = _attn_out(qkv, band_bias, bg, gb, za, x2, w_ob[l], w_out[l],
                       bsz=bsz, seq=seq)
    return x2.reshape(bsz, seq, d)
```

```python
import functools

import jax
import jax.numpy as jnp
from jax import lax
from jax.experimental import pallas as pl
from jax.experimental.pallas import tpu as pltpu

F32 = jnp.float32
BF16 = jnp.bfloat16

EPS = 1e-6
NEG = -1e30
CHUNK = 128
A_GROUPS = 4
B_PATTERNS = ((128, 1), (512, 4), (2048, 16))
B_HEADS = 8
B_HEAD_DIM = 64
B_WIDTH = B_HEADS * B_HEAD_DIM
LANES = 128
N_PAIRS = B_WIDTH // LANES
ATTN_BLOCK = 128
SPAN = 2048
UNITS = SPAN // ATTN_BLOCK
Q_SCALE = B_HEAD_DIM ** -0.5 * 1.4426950408889634

VMEM_LIMIT_BYTES = 60 * 1024 * 1024


def _params(*semantics):
    return pltpu.CompilerParams(dimension_semantics=semantics, vmem_limit_bytes=VMEM_LIMIT_BYTES)


def _resident(shape):
    return pl.BlockSpec(shape, lambda *_: (0,) * len(shape), pipeline_mode=pl.Buffered(1))


WEIGHT_SLOTS = 3
WEIGHT_CHUNK_BYTES = 3 << 19


def _weight_scratch(shape):
    rows, cols = shape
    chunk = 16
    while 2 * chunk * cols * 4 <= WEIGHT_CHUNK_BYTES and rows % (2 * chunk) == 0:
        chunk *= 2
    return [pltpu.VMEM((WEIGHT_SLOTS, chunk, cols), F32), pltpu.SemaphoreType.DMA((WEIGHT_SLOTS,))]


def _load_weight_bf16(src_hbm, dst_ref, stage_ref, sem):
    slots, chunk, _ = stage_ref.shape
    n = src_hbm.shape[0] // chunk

    def copy(c):
        return pltpu.make_async_copy(src_hbm.at[pl.ds(c * chunk, chunk)], stage_ref.at[c % slots],
                                     sem.at[c % slots])

    for c in range(min(slots - 1, n)):
        copy(c).start()
    for c in range(n):
        if c + slots - 1 < n:
            copy(c + slots - 1).start()
        copy(c).wait()
        dst_ref[c * chunk:(c + 1) * chunk, :] = stage_ref[c % slots].astype(BF16)


def _head_rms(t, gain):
    low = lax.broadcasted_iota(jnp.int32, (1, LANES), 1) < B_HEAD_DIM
    cols = []
    for cb in range(N_PAIRS):
        blk = t[:, cb * LANES:(cb + 1) * LANES]
        sq = blk * blk
        s0 = jnp.sum(jnp.where(low, sq, 0.0), axis=-1, keepdims=True)
        s1 = jnp.sum(jnp.where(low, 0.0, sq), axis=-1, keepdims=True)
        ms = jnp.where(low, s0, s1) * (1.0 / B_HEAD_DIM)
        cols.append(blk * lax.rsqrt(ms + EPS) * gain[:, cb * LANES:(cb + 1) * LANES])
    return jnp.concatenate(cols, axis=1)


def _emit_residue_major(val, out_ref):
    dilation, rows = out_ref.shape[1], out_ref.shape[2]
    for hp in range(N_PAIRS):
        for r in range(dilation):
            out_ref[hp, r] = val[r * rows:(r + 1) * rows, hp * LANES:(hp + 1) * LANES].astype(out_ref.dtype)


def _fused_in_kernel(x_ref, ng_ref, win_ref, ws_ref, sb_ref, lng_ref, lnb_ref, woa_ref, gq_ref, gk_ref,
                     za_ref, gb_ref, bg_ref, q0_ref, k0_ref, v0_ref, q1_ref, k1_ref, v1_ref,
                     q2_ref, k2_ref, v2_ref, h_ref, h4_ref, h16_ref, xs_ref, s_ref, vb_ref, yp_ref,
                     win_bf_ref, woa_bf_ref, win_stage_ref, win_sem, woa_stage_ref, woa_sem):
    tm, d_model = x_ref.shape
    a_width = woa_ref.shape[0]

    @pl.when(pl.program_id(0) == 0)
    def _():
        _load_weight_bf16(win_ref, win_bf_ref, win_stage_ref, win_sem)
        _load_weight_bf16(woa_ref, woa_bf_ref, woa_stage_ref, woa_sem)

    gw = a_width // A_GROUPS
    cbw = 512
    col_u, col_v, col_g = 0, a_width, 2 * a_width
    col_qkv = 3 * a_width
    col_bg = col_qkv + 3 * len(B_PATTERNS) * B_WIDTH
    col_ga = col_bg + B_WIDTH
    col_gb = col_ga + d_model

    x = x_ref[...]
    ms = jnp.mean(x * x, axis=-1, keepdims=True)
    xn = x * lax.rsqrt(ms + EPS) * ng_ref[...]
    h_ref[...] = xn.astype(BF16)
    for s in range(d_model // LANES):
        xs_ref[s] = xn[:, s * LANES:(s + 1) * LANES]
    h_by_dilation = {1: h_ref, 4: h4_ref, 16: h16_ref}
    for dil, hp_ref in ((4, h4_ref), (16, h16_ref)):
        rows = tm // dil
        for s in range(d_model // LANES):
            for r in range(dil):
                hp_ref[r * rows:(r + 1) * rows, s * LANES:(s + 1) * LANES] = (
                    xs_ref[s, pl.ds(r, rows, stride=dil), :].astype(BF16))

    def proj(col, width, src_ref=h_ref):
        return jnp.dot(src_ref[...], win_bf_ref[:, col:col + width], preferred_element_type=F32)

    outs = ((q0_ref, k0_ref, v0_ref), (q1_ref, k1_ref, v1_ref), (q2_ref, k2_ref, v2_ref))

    def emit_qkv(p):
        q_ref, k_ref, v_ref = outs[p]
        hsrc = h_by_dilation[B_PATTERNS[p][1]]
        base = col_qkv + p * 3 * B_WIDTH
        _emit_residue_major(_head_rms(proj(base, B_WIDTH, hsrc), gq_ref[p] * Q_SCALE), q_ref)
        _emit_residue_major(_head_rms(proj(base + B_WIDTH, B_WIDTH, hsrc), gk_ref[p]), k_ref)
        _emit_residue_major(proj(base + 2 * B_WIDTH, B_WIDTH, hsrc), v_ref)

    for cb in range(a_width // cbw):
        s_ref[:, cb * cbw:(cb + 1) * cbw] = jax.nn.gelu(proj(col_v + cb * cbw, cbw))
    emit_qkv(0)
    v = s_ref[...]
    mu = jnp.mean(v, axis=-1, keepdims=True)
    vc = v - mu
    var = jnp.mean(vc * vc, axis=-1, keepdims=True)
    vb_ref[...] = (vc * lax.rsqrt(var + EPS) * lng_ref[...] + lnb_ref[...]).astype(BF16)
    emit_qkv(1)
    row = lax.broadcasted_iota(jnp.int32, (CHUNK, CHUNK), 0)
    col = lax.broadcasted_iota(jnp.int32, (CHUNK, CHUNK), 1)
    for g in range(A_GROUPS):
        w = jnp.where(row >= col, ws_ref[g], 0.0).astype(BF16)
        gs = slice(g * gw, (g + 1) * gw)
        for c in range(tm // CHUNK):
            rs = slice(c * CHUNK, (c + 1) * CHUNK)
            s_ref[rs, gs] = jnp.dot(w, vb_ref[rs, gs], preferred_element_type=F32) + sb_ref[:, gs]
    emit_qkv(2)
    bg = jax.nn.silu(proj(col_bg, B_WIDTH))
    for hp in range(N_PAIRS):
        bg_ref[hp] = bg[:, hp * LANES:(hp + 1) * LANES].astype(bg_ref.dtype)
    for cb in range(a_width // cbw):
        cs = slice(cb * cbw, (cb + 1) * cbw)
        u = jax.nn.gelu(proj(col_u + cb * cbw, cbw))
        gate = jax.nn.silu(proj(col_g + cb * cbw, cbw))
        yp_ref[:, cs] = (u * s_ref[:, cs] * gate).astype(BF16)
    for cb in range(d_model // cbw):
        cs = slice(cb * cbw, (cb + 1) * cbw)
        gb_ref[:, cs] = jax.nn.sigmoid(proj(col_gb + cb * cbw, cbw)).astype(gb_ref.dtype)
        ya = jnp.dot(yp_ref[...], woa_bf_ref[:, cs], preferred_element_type=F32)
        za_ref[:, cs] = (jax.nn.sigmoid(proj(col_ga + cb * cbw, cbw)) * ya).astype(za_ref.dtype)


def _fused_in(x2, ng, win, ws, sb, lng, lnb, woa, gq, gk, *, bsz, seq, tm=512):
    t, d = x2.shape
    a_width = woa.shape[0]
    tiles_per_seq = seq // tm
    row = lambda width: pl.BlockSpec((tm, width), lambda i: (i, 0))
    hbm = pl.BlockSpec(memory_space=pl.ANY)
    out_specs = [row(d), row(d), pl.BlockSpec((N_PAIRS, tm, LANES), lambda i: (0, i, 0))]
    out_shape = [jax.ShapeDtypeStruct((t, d), BF16), jax.ShapeDtypeStruct((t, d), BF16),
                 jax.ShapeDtypeStruct((N_PAIRS, t, LANES), BF16)]
    for _, dil in B_PATTERNS:
        spec = pl.BlockSpec((None, N_PAIRS, dil, tm // dil, LANES),
                            lambda i: (i // tiles_per_seq, 0, 0, i % tiles_per_seq, 0))
        shape = jax.ShapeDtypeStruct((bsz, N_PAIRS, dil, seq // dil, LANES), BF16)
        out_specs += [spec] * 3
        out_shape += [shape] * 3
    return pl.pallas_call(
        _fused_in_kernel,
        grid=(t // tm,),
        in_specs=[row(d), _resident(ng.shape), hbm, _resident(ws.shape),
                  _resident(sb.shape), _resident(lng.shape), _resident(lnb.shape), hbm,
                  _resident(gq.shape), _resident(gk.shape)],
        out_specs=out_specs,
        out_shape=out_shape,
        scratch_shapes=[pltpu.VMEM((tm, d), BF16), pltpu.VMEM((tm, d), BF16), pltpu.VMEM((tm, d), BF16),
                        pltpu.VMEM((d // LANES, tm, LANES), F32), pltpu.VMEM((tm, a_width), F32),
                        pltpu.VMEM((tm, a_width), BF16), pltpu.VMEM((tm, a_width), BF16),
                        pltpu.VMEM(win.shape, BF16), pltpu.VMEM(woa.shape, BF16)]
        + _weight_scratch(win.shape) + _weight_scratch(woa.shape),
        compiler_params=_params("arbitrary"),
        name="fused_in",
    )(x2, ng, win, ws, sb, lng, lnb, woa, gq, gk)


def _attn_unit(q2, k2, v2, bias, low):
    n = ATTN_BLOCK
    v2e = jnp.concatenate([v2, jnp.ones_like(v2)], axis=1)
    zero = jnp.zeros_like(q2)
    qs = jnp.concatenate([jnp.where(low, q2, zero), jnp.where(low, zero, q2)], axis=0)
    s = lax.dot_general(qs, k2, (((1,), (1,)), ((), ())), preferred_element_type=F32)
    s = s + jnp.concatenate([bias, bias], axis=0)
    m = jnp.max(s, axis=-1, keepdims=True)
    e = jnp.exp2(s - m).astype(BF16)
    oe = jnp.dot(e, v2e, preferred_element_type=F32)
    mb = jnp.broadcast_to(m, (2 * n, LANES))
    return (jnp.where(low, oe[:n, :LANES], oe[n:, :LANES]),
            jnp.where(low, oe[:n, LANES:], oe[n:, LANES:]),
            jnp.where(low, mb[:n], mb[n:]))


def _attn_out_kernel(q0_ref, k0_ref, v0_ref, kp0_ref, vp0_ref, q1_ref, k1_ref, v1_ref, kp1_ref, vp1_ref,
                     q2_ref, k2_ref, v2_ref, kp2_ref, vp2_ref, bias_ref, bg_ref,
                     gb_ref, za_ref, x_ref, wob_ref, wout_ref, out_ref,
                     num_ref, den_ref, max_ref, yb_ref, wob_bf_ref, wout_bf_ref, stage_ref, sem,
                     *, n_spans, spans_per_seq):
    s = pl.program_id(0)
    hp = pl.program_id(1)
    tq = out_ref.shape[0]

    @pl.when(jnp.logical_and(s == 0, hp == 0))
    def _():
        yb_ref[...] = jnp.zeros_like(yb_ref)
        _load_weight_bf16(wob_ref, wob_bf_ref, stage_ref, sem)
        _load_weight_bf16(wout_ref, wout_bf_ref, stage_ref, sem)

    def out_projection(lo, n):
        src_rows = pl.ds(pl.multiple_of(hp * tq + lo, n), n)
        yb = jnp.concatenate([yb_ref[(s + 1) % 2, j, src_rows, :] for j in range(N_PAIRS)], axis=1)
        yb = jnp.dot(yb, wob_bf_ref[...], preferred_element_type=F32)
        rs = slice(lo, lo + n)
        merged = za_ref[rs, :].astype(F32) + gb_ref[rs, :].astype(F32) * yb
        out_ref[rs, :] = x_ref[rs, :] + jnp.dot(merged.astype(BF16), wout_bf_ref[...],
                                                 preferred_element_type=F32)

    @pl.when(s == n_spans)
    def _():
        out_projection(0, tq)

    @pl.when(s < n_spans)
    def _():
        first_span = s % spans_per_seq == 0
        low = lax.broadcasted_iota(jnp.int32, (1, LANES), 1) < B_HEAD_DIM
        pats = ((q0_ref, k0_ref, v0_ref, kp0_ref, vp0_ref),
                (q1_ref, k1_ref, v1_ref, kp1_ref, vp1_ref),
                (q2_ref, k2_ref, v2_ref, kp2_ref, vp2_ref))
        def unit(p, r, c):
            q_ref, k_ref, v_ref, kp_ref, vp_ref = pats[p]
            dil = B_PATTERNS[p][1]
            rows = slice(c * ATTN_BLOCK, (c + 1) * ATTN_BLOCK)
            if c == 0:
                k2 = jnp.concatenate([kp_ref[r], k_ref[r, rows, :]], axis=0)
                v2 = jnp.concatenate([vp_ref[r], v_ref[r, rows, :]], axis=0)
                bias = bias_ref[jnp.where(first_span, 0, 1)]
            else:
                both = slice((c - 1) * ATTN_BLOCK, (c + 1) * ATTN_BLOCK)
                k2, v2 = k_ref[r, both, :], v_ref[r, both, :]
                bias = bias_ref[1]
            num, den, mx = _attn_unit(q_ref[r, rows, :], k2, v2, bias, low)
            dst = pl.ds(c * (ATTN_BLOCK * dil) + r, ATTN_BLOCK, stride=dil)
            num_ref[p, dst, :] = num
            den_ref[p, dst, :] = den
            max_ref[p, dst, :] = mx

        def combine(rs):
            m0, m1, m2 = max_ref[0, rs], max_ref[1, rs], max_ref[2, rs]
            mm = jnp.maximum(jnp.maximum(m0, m1), m2)
            a0, a1, a2 = jnp.exp2(m0 - mm), jnp.exp2(m1 - mm), jnp.exp2(m2 - mm)
            num = a0 * num_ref[0, rs] + a1 * num_ref[1, rs] + a2 * num_ref[2, rs]
            den = a0 * den_ref[0, rs] + a1 * den_ref[1, rs] + a2 * den_ref[2, rs]
            yb_ref[s % 2, hp, rs, :] = (num / den * bg_ref[rs, :].astype(F32)).astype(yb_ref.dtype)

        p16, p4, p1 = 2, 1, 0
        assert [B_PATTERNS[p][1] for p in (p16, p4, p1)] == [16, 4, 1]
        parts = 2
        for part in range(parts):
            out_projection(part * (tq // parts), tq // parts)
            for r in range(part * (16 // parts), (part + 1) * (16 // parts)):
                unit(p16, r, 0)
        for c4 in range(UNITS // 4):
            for r in range(4):
                unit(p4, r, c4)
            for c in range(4 * c4, 4 * c4 + 4):
                unit(p1, 0, c)
                if c % 2 == 1:
                    combine(slice((c - 1) * ATTN_BLOCK, (c + 1) * ATTN_BLOCK))


def _attn_out(qkv, bias, bg, gb, za, x2, wob, wout, *, bsz, seq):
    spans = seq // SPAN
    n_spans = bsz * spans
    t, d = x2.shape
    tq = SPAN // N_PAIRS

    def span_of(s):
        sc = jnp.minimum(s, n_spans - 1)
        return sc // spans, sc % spans

    in_specs = []
    for _, dil in B_PATTERNS:
        rows = SPAN // dil

        def cur_map(s, hp):
            b, c = span_of(s)
            return (b, hp, 0, c, 0)

        def prev_map(s, hp, rows=rows):
            b, c = span_of(s)
            return (b, hp, 0, jnp.maximum(c * (rows // ATTN_BLOCK) - 1, 0), 0)

        cur = pl.BlockSpec((None, None, dil, rows, LANES), cur_map)
        prev = pl.BlockSpec((None, None, dil, ATTN_BLOCK, LANES), prev_map)
        in_specs += [cur, cur, cur, prev, prev]
    tail = pl.BlockSpec((tq, d), lambda s, hp: (jnp.maximum((s - 1) * N_PAIRS + hp, 0), 0))
    in_specs += [pl.BlockSpec(bias.shape, lambda s, hp: (0, 0, 0)),
                 pl.BlockSpec((None, SPAN, LANES), lambda s, hp: (hp, jnp.minimum(s, n_spans - 1), 0)),
                 tail, tail, tail, pl.BlockSpec(memory_space=pl.ANY), pl.BlockSpec(memory_space=pl.ANY)]
    args = []
    for q, k, v in qkv:
        args += [q, k, v, k, v]
    return pl.pallas_call(
        functools.partial(_attn_out_kernel, n_spans=n_spans, spans_per_seq=spans),
        grid=(n_spans + 1, N_PAIRS),
        in_specs=in_specs,
        out_specs=tail,
        out_shape=jax.ShapeDtypeStruct((t, d), F32),
        scratch_shapes=[pltpu.VMEM((len(B_PATTERNS), SPAN, LANES), F32)] * 3
        + [pltpu.VMEM((2, N_PAIRS, SPAN, LANES), BF16), pltpu.VMEM(wob.shape, BF16), pltpu.VMEM(wout.shape, BF16)]
        + _weight_scratch(wout.shape),
        compiler_params=_params("arbitrary", "arbitrary"),
        name="attn_out",
    )(*args, bias, bg, gb, za, x2, wob, wout)


def _band_bias():
    qi = lax.broadcasted_iota(jnp.int32, (ATTN_BLOCK, 2 * ATTN_BLOCK), 0)
    kj = lax.broadcasted_iota(jnp.int32, (ATTN_BLOCK, 2 * ATTN_BLOCK), 1)
    dist = qi + ATTN_BLOCK - kj
    band = (dist >= 0) & (dist <= ATTN_BLOCK)
    first = band & (kj >= ATTN_BLOCK)
    return jnp.stack([jnp.where(first, 0.0, NEG), jnp.where(band, 0.0, NEG)]).astype(F32)


def kernel(x, norm_g, w_in, a_ws, a_bs, a_ln_g, a_ln_b, b_qn_g, b_kn_g, w_oa, w_ob, w_out):
    bsz, seq, d = x.shape
    depth = w_in.shape[0]
    a_width = w_oa.shape[1]
    npat = len(B_PATTERNS)
    assert w_in.shape[2] == 3 * a_width + 3 * npat * B_WIDTH + B_WIDTH + 2 * d
    assert all(w // dil == ATTN_BLOCK and SPAN % (ATTN_BLOCK * dil) == 0 for w, dil in B_PATTERNS)
    assert seq % SPAN == 0
    t = bsz * seq
    band_bias = _band_bias()
    x2 = x.reshape(t, d)
    for l in range(depth):
        sgu_bias = jnp.repeat(a_bs[l].T, a_width // A_GROUPS, axis=1)
        gq = jnp.tile(b_qn_g[l], (1, B_HEADS)).reshape(npat, 1, B_WIDTH)
        gk = jnp.tile(b_kn_g[l], (1, B_HEADS)).reshape(npat, 1, B_WIDTH)
        res = _fused_in(x2, norm_g[l].reshape(1, d), w_in[l], a_ws[l], sgu_bias,
                        a_ln_g[l].reshape(1, -1), a_ln_b[l].reshape(1, -1), w_oa[l],
                        gq, gk, bsz=bsz, seq=seq)
        za, gb, bg = res[:3]
        qkv = [res[3 + 3 * p:6 + 3 * p] for p in range(npat)]
        x2 = _attn_out(qkv, band_bias, bg, gb, za, x2, w_ob[l], w_out[l],
                       bsz=bsz, seq=seq)
    return x2.reshape(bsz, seq, d)
```

```python
import functools

import jax
import jax.numpy as jnp
from jax import lax
from jax.experimental import pallas as pl
from jax.experimental.pallas import tpu as pltpu

F32 = jnp.float32
BF16 = jnp.bfloat16

EPS = 1e-6
NEG = -1e30
CHUNK = 128
A_GROUPS = 4
B_PATTERNS = ((128, 1), (512, 4), (2048, 16))
B_HEADS = 8
B_HEAD_DIM = 64
B_WIDTH = B_HEADS * B_HEAD_DIM
LANES = 128
N_PAIRS = B_WIDTH // LANES
ATTN_BLOCK = 128
SPAN = 2048
UNITS = SPAN // ATTN_BLOCK
Q_SCALE = B_HEAD_DIM ** -0.5 * 1.4426950408889634

VMEM_LIMIT_BYTES = 60 * 1024 * 1024


def _params(*semantics):
    return pltpu.CompilerParams(dimension_semantics=semantics, vmem_limit_bytes=VMEM_LIMIT_BYTES)


def _resident(shape):
    return pl.BlockSpec(shape, lambda *_: (0,) * len(shape), pipeline_mode=pl.Buffered(1))


WEIGHT_SLOTS = 3
WEIGHT_CHUNK_BYTES = 3 << 19


def _weight_scratch(shape):
    rows, cols = shape
    chunk = 16
    while 2 * chunk * cols * 4 <= WEIGHT_CHUNK_BYTES and rows % (2 * chunk) == 0:
        chunk *= 2
    return [pltpu.VMEM((WEIGHT_SLOTS, chunk, cols), F32), pltpu.SemaphoreType.DMA((WEIGHT_SLOTS,))]


def _load_weight_bf16(src_hbm, dst_ref, stage_ref, sem):
    slots, chunk, _ = stage_ref.shape
    n = src_hbm.shape[0] // chunk

    def copy(c):
        return pltpu.make_async_copy(src_hbm.at[pl.ds(c * chunk, chunk)], stage_ref.at[c % slots],
                                     sem.at[c % slots])

    for c in range(min(slots - 1, n)):
        copy(c).start()
    for c in range(n):
        if c + slots - 1 < n:
            copy(c + slots - 1).start()
        copy(c).wait()
        dst_ref[c * chunk:(c + 1) * chunk, :] = stage_ref[c % slots].astype(BF16)


def _head_rms(t, gain):
    low = lax.broadcasted_iota(jnp.int32, (1, LANES), 1) < B_HEAD_DIM
    cols = []
    for cb in range(N_PAIRS):
        blk = t[:, cb * LANES:(cb + 1) * LANES]
        sq = blk * blk
        s0 = jnp.sum(jnp.where(low, sq, 0.0), axis=-1, keepdims=True)
        s1 = jnp.sum(jnp.where(low, 0.0, sq), axis=-1, keepdims=True)
        ms = jnp.where(low, s0, s1) * (1.0 / B_HEAD_DIM)
        cols.append(blk * lax.rsqrt(ms + EPS) * gain[:, cb * LANES:(cb + 1) * LANES])
    return jnp.concatenate(cols, axis=1)


def _emit_residue_major(val, out_ref, part, n_parts):
    dilation = out_ref.shape[1]
    rows = out_ref.shape[2] // n_parts
    for hp in range(N_PAIRS):
        for r in range(dilation):
            out_ref[hp, r, part * rows:(part + 1) * rows, :] = (
                val[r * rows:(r + 1) * rows, hp * LANES:(hp + 1) * LANES].astype(out_ref.dtype))


FUSED_IN_PARTS = 2


def _fused_in_kernel(x_ref, ng_ref, win_ref, ws_ref, sb_ref, lng_ref, lnb_ref, woa_ref, gq_ref, gk_ref,
                     za_ref, gb_ref, bg_ref, q0_ref, k0_ref, v0_ref, q1_ref, k1_ref, v1_ref,
                     q2_ref, k2_ref, v2_ref, h_ref, h4_ref, h16_ref, xs_ref, s_ref, vb_ref, yp_ref,
                     win_bf_ref, woa_bf_ref, win_stage_ref, win_sem, woa_stage_ref, woa_sem):
    tm, d_model = x_ref.shape
    a_width = woa_ref.shape[0]
    n_parts = h_ref.shape[0]
    th = tm // n_parts

    @pl.when(pl.program_id(0) == 0)
    def _():
        _load_weight_bf16(win_ref, win_bf_ref, win_stage_ref, win_sem)
        _load_weight_bf16(woa_ref, woa_bf_ref, woa_stage_ref, woa_sem)

    gw = a_width // A_GROUPS
    cbw = 512
    col_u, col_v, col_g = 0, a_width, 2 * a_width
    col_qkv = 3 * a_width
    col_bg = col_qkv + 3 * len(B_PATTERNS) * B_WIDTH
    col_ga = col_bg + B_WIDTH
    col_gb = col_ga + d_model
    h_by_dilation = {1: h_ref, 4: h4_ref, 16: h16_ref}
    outs = ((q0_ref, k0_ref, v0_ref), (q1_ref, k1_ref, v1_ref), (q2_ref, k2_ref, v2_ref))
    row = lax.broadcasted_iota(jnp.int32, (CHUNK, CHUNK), 0)
    col = lax.broadcasted_iota(jnp.int32, (CHUNK, CHUNK), 1)

    def proj(pt, col0, width, src_ref=h_ref):
        return jnp.dot(src_ref[pt], win_bf_ref[:, col0:col0 + width], preferred_element_type=F32)

    def rows_of(pt):
        return slice(pt * th, (pt + 1) * th)

    def stage_norm(pt):
        x = x_ref[rows_of(pt), :]
        ms = jnp.mean(x * x, axis=-1, keepdims=True)
        xn = x * lax.rsqrt(ms + EPS) * ng_ref[...]
        h_ref[pt] = xn.astype(BF16)
        for s in range(d_model // LANES):
            xs_ref[pt, s] = xn[:, s * LANES:(s + 1) * LANES]
        for dil, hp_ref in ((4, h4_ref), (16, h16_ref)):
            rows = th // dil
            for s in range(d_model // LANES):
                for r in range(dil):
                    hp_ref[pt, r * rows:(r + 1) * rows, s * LANES:(s + 1) * LANES] = (
                        xs_ref[pt, s, pl.ds(r, rows, stride=dil), :].astype(BF16))

    def stage_v(pt):
        for cb in range(a_width // cbw):
            s_ref[pt, :, cb * cbw:(cb + 1) * cbw] = jax.nn.gelu(proj(pt, col_v + cb * cbw, cbw))

    def stage_qkv(pt, p):
        q_ref, k_ref, v_ref = outs[p]
        hsrc = h_by_dilation[B_PATTERNS[p][1]]
        base = col_qkv + p * 3 * B_WIDTH
        _emit_residue_major(_head_rms(proj(pt, base, B_WIDTH, hsrc), gq_ref[p] * Q_SCALE), q_ref, pt, n_parts)
        _emit_residue_major(_head_rms(proj(pt, base + B_WIDTH, B_WIDTH, hsrc), gk_ref[p]), k_ref, pt, n_parts)
        _emit_residue_major(proj(pt, base + 2 * B_WIDTH, B_WIDTH, hsrc), v_ref, pt, n_parts)

    def stage_layernorm(pt):
        v = s_ref[pt]
        mu = jnp.mean(v, axis=-1, keepdims=True)
        vc = v - mu
        var = jnp.mean(vc * vc, axis=-1, keepdims=True)
        vb_ref[pt] = (vc * lax.rsqrt(var + EPS) * lng_ref[...] + lnb_ref[...]).astype(BF16)

    def stage_spatial(pt):
        for g in range(A_GROUPS):
            w = jnp.where(row >= col, ws_ref[g], 0.0).astype(BF16)
            gs = slice(g * gw, (g + 1) * gw)
            for c in range(th // CHUNK):
                rs = slice(c * CHUNK, (c + 1) * CHUNK)
                s_ref[pt, rs, gs] = (jnp.dot(w, vb_ref[pt, rs, gs], preferred_element_type=F32)
                                     + sb_ref[:, gs])

    def stage_bgate(pt):
        bg = jax.nn.silu(proj(pt, col_bg, B_WIDTH))
        for hp in range(N_PAIRS):
            bg_ref[hp, rows_of(pt), :] = bg[:, hp * LANES:(hp + 1) * LANES].astype(bg_ref.dtype)

    def stage_gates(pt):
        for cb in range(a_width // cbw):
            cs = slice(cb * cbw, (cb + 1) * cbw)
            u = jax.nn.gelu(proj(pt, col_u + cb * cbw, cbw))
            gate = jax.nn.silu(proj(pt, col_g + cb * cbw, cbw))
            yp_ref[pt, :, cs] = (u * s_ref[pt, :, cs] * gate).astype(BF16)

    def stage_out(pt):
        for cb in range(d_model // cbw):
            cs = slice(cb * cbw, (cb + 1) * cbw)
            gb_ref[rows_of(pt), cs] = jax.nn.sigmoid(proj(pt, col_gb + cb * cbw, cbw)).astype(gb_ref.dtype)
            ya = jnp.dot(yp_ref[pt], woa_bf_ref[:, cs], preferred_element_type=F32)
            za_ref[rows_of(pt), cs] = (jax.nn.sigmoid(proj(pt, col_ga + cb * cbw, cbw)) * ya).astype(za_ref.dtype)

    stages = (stage_norm, stage_v, lambda pt: stage_qkv(pt, 0), stage_layernorm,
              lambda pt: stage_qkv(pt, 1), stage_spatial,
              lambda pt: (stage_qkv(pt, 2), stage_bgate(pt)), stage_gates, stage_out)
    for k in range(len(stages) + n_parts - 1):
        for pt in range(n_parts):
            if 0 <= k - pt < len(stages):
                stages[k - pt](pt)


def _fused_in(x2, ng, win, ws, sb, lng, lnb, woa, gq, gk, *, bsz, seq, tm=512):
    t, d = x2.shape
    a_width = woa.shape[0]
    tiles_per_seq = seq // tm
    parts, th = FUSED_IN_PARTS, tm // FUSED_IN_PARTS
    row = lambda width: pl.BlockSpec((tm, width), lambda i: (i, 0))
    hbm = pl.BlockSpec(memory_space=pl.ANY)
    out_specs = [row(d), row(d), pl.BlockSpec((N_PAIRS, tm, LANES), lambda i: (0, i, 0))]
    out_shape = [jax.ShapeDtypeStruct((t, d), BF16), jax.ShapeDtypeStruct((t, d), BF16),
                 jax.ShapeDtypeStruct((N_PAIRS, t, LANES), BF16)]
    for _, dil in B_PATTERNS:
        spec = pl.BlockSpec((None, N_PAIRS, dil, tm // dil, LANES),
                            lambda i: (i // tiles_per_seq, 0, 0, i % tiles_per_seq, 0))
        shape = jax.ShapeDtypeStruct((bsz, N_PAIRS, dil, seq // dil, LANES), BF16)
        out_specs += [spec] * 3
        out_shape += [shape] * 3
    return pl.pallas_call(
        _fused_in_kernel,
        grid=(t // tm,),
        in_specs=[row(d), _resident(ng.shape), hbm, _resident(ws.shape),
                  _resident(sb.shape), _resident(lng.shape), _resident(lnb.shape), hbm,
                  _resident(gq.shape), _resident(gk.shape)],
        out_specs=out_specs,
        out_shape=out_shape,
        scratch_shapes=[pltpu.VMEM((parts, th, d), BF16), pltpu.VMEM((parts, th, d), BF16),
                        pltpu.VMEM((parts, th, d), BF16), pltpu.VMEM((parts, d // LANES, th, LANES), F32),
                        pltpu.VMEM((parts, th, a_width), F32), pltpu.VMEM((parts, th, a_width), BF16),
                        pltpu.VMEM((parts, th, a_width), BF16),
                        pltpu.VMEM(win.shape, BF16), pltpu.VMEM(woa.shape, BF16)]
        + _weight_scratch(win.shape) + _weight_scratch(woa.shape),
        compiler_params=_params("arbitrary"),
        name="fused_in",
    )(x2, ng, win, ws, sb, lng, lnb, woa, gq, gk)


def _attn_unit(q2, k2, v2, bias, low):
    n = ATTN_BLOCK
    v2e = jnp.concatenate([v2, jnp.ones_like(v2)], axis=1)
    zero = jnp.zeros_like(q2)
    qs = jnp.concatenate([jnp.where(low, q2, zero), jnp.where(low, zero, q2)], axis=0)
    s = lax.dot_general(qs, k2, (((1,), (1,)), ((), ())), preferred_element_type=F32)
    s = s + jnp.concatenate([bias, bias], axis=0)
    m = jnp.max(s, axis=-1, keepdims=True)
    e = jnp.exp2(s - m).astype(BF16)
    oe = jnp.dot(e, v2e, preferred_element_type=F32)
    mb = jnp.broadcast_to(m, (2 * n, LANES))
    return (jnp.where(low, oe[:n, :LANES], oe[n:, :LANES]),
            jnp.where(low, oe[:n, LANES:], oe[n:, LANES:]),
            jnp.where(low, mb[:n], mb[n:]))


def _attn_out_kernel(q0_ref, k0_ref, v0_ref, kp0_ref, vp0_ref, q1_ref, k1_ref, v1_ref, kp1_ref, vp1_ref,
                     q2_ref, k2_ref, v2_ref, kp2_ref, vp2_ref, bias_ref, bg_ref,
                     gb_ref, za_ref, x_ref, wob_ref, wout_ref, out_ref,
                     num_ref, den_ref, max_ref, yb_ref, wob_bf_ref, wout_bf_ref, stage_ref, sem,
                     *, n_spans, spans_per_seq):
    s = pl.program_id(0)
    hp = pl.program_id(1)
    tq = out_ref.shape[0]

    @pl.when(jnp.logical_and(s == 0, hp == 0))
    def _():
        yb_ref[...] = jnp.zeros_like(yb_ref)
        _load_weight_bf16(wob_ref, wob_bf_ref, stage_ref, sem)
        _load_weight_bf16(wout_ref, wout_bf_ref, stage_ref, sem)

    def out_projection(lo, n):
        src_rows = pl.ds(pl.multiple_of(hp * tq + lo, n), n)
        yb = jnp.concatenate([yb_ref[(s + 1) % 2, j, src_rows, :] for j in range(N_PAIRS)], axis=1)
        yb = jnp.dot(yb, wob_bf_ref[...], preferred_element_type=F32)
        rs = slice(lo, lo + n)
        merged = za_ref[rs, :].astype(F32) + gb_ref[rs, :].astype(F32) * yb
        out_ref[rs, :] = x_ref[rs, :] + jnp.dot(merged.astype(BF16), wout_bf_ref[...],
                                                 preferred_element_type=F32)

    @pl.when(s == n_spans)
    def _():
        out_projection(0, tq)

    @pl.when(s < n_spans)
    def _():
        first_span = s % spans_per_seq == 0
        low = lax.broadcasted_iota(jnp.int32, (1, LANES), 1) < B_HEAD_DIM
        pats = ((q0_ref, k0_ref, v0_ref, kp0_ref, vp0_ref),
                (q1_ref, k1_ref, v1_ref, kp1_ref, vp1_ref),
                (q2_ref, k2_ref, v2_ref, kp2_ref, vp2_ref))
        def unit(p, r, c):
            q_ref, k_ref, v_ref, kp_ref, vp_ref = pats[p]
            dil = B_PATTERNS[p][1]
            rows = slice(c * ATTN_BLOCK, (c + 1) * ATTN_BLOCK)
            if c == 0:
                k2 = jnp.concatenate([kp_ref[r], k_ref[r, rows, :]], axis=0)
                v2 = jnp.concatenate([vp_ref[r], v_ref[r, rows, :]], axis=0)
                bias = bias_ref[jnp.where(first_span, 0, 1)]
            else:
                both = slice((c - 1) * ATTN_BLOCK, (c + 1) * ATTN_BLOCK)
                k2, v2 = k_ref[r, both, :], v_ref[r, both, :]
                bias = bias_ref[1]
            num, den, mx = _attn_unit(q_ref[r, rows, :], k2, v2, bias, low)
            dst = pl.ds(c * (ATTN_BLOCK * dil) + r, ATTN_BLOCK, stride=dil)
            num_ref[p, dst, :] = num
            den_ref[p, dst, :] = den
            max_ref[p, dst, :] = mx

        def combine(rs):
            m0, m1, m2 = max_ref[0, rs], max_ref[1, rs], max_ref[2, rs]
            mm = jnp.maximum(jnp.maximum(m0, m1), m2)
            a0, a1, a2 = jnp.exp2(m0 - mm), jnp.exp2(m1 - mm), jnp.exp2(m2 - mm)
            num = a0 * num_ref[0, rs] + a1 * num_ref[1, rs] + a2 * num_ref[2, rs]
            den = a0 * den_ref[0, rs] + a1 * den_ref[1, rs] + a2 * den_ref[2, rs]
            yb_ref[s % 2, hp, rs, :] = (num / den * bg_ref[rs, :].astype(F32)).astype(yb_ref.dtype)

        p16, p4, p1 = 2, 1, 0
        assert [B_PATTERNS[p][1] for p in (p16, p4, p1)] == [16, 4, 1]
        parts = 2
        for part in range(parts):
            out_projection(part * (tq // parts), tq // parts)
            for r in range(part * (16 // parts), (part + 1) * (16 // parts)):
                unit(p16, r, 0)
        for c4 in range(UNITS // 4):
            for r in range(4):
                unit(p4, r, c4)
            for c in range(4 * c4, 4 * c4 + 4):
                unit(p1, 0, c)
                if c % 2 == 1:
                    combine(slice((c - 1) * ATTN_BLOCK, (c + 1) * ATTN_BLOCK))


def _attn_out(qkv, bias, bg, gb, za, x2, wob, wout, *, bsz, seq):
    spans = seq // SPAN
    n_spans = bsz * spans
    t, d = x2.shape
    tq = SPAN // N_PAIRS

    def span_of(s):
        sc = jnp.minimum(s, n_spans - 1)
        return sc // spans, sc % spans

    in_specs = []
    for _, dil in B_PATTERNS:
        rows = SPAN // dil

        def cur_map(s, hp):
            b, c = span_of(s)
            return (b, hp, 0, c, 0)

        def prev_map(s, hp, rows=rows):
            b, c = span_of(s)
            return (b, hp, 0, jnp.maximum(c * (rows // ATTN_BLOCK) - 1, 0), 0)

        cur = pl.BlockSpec((None, None, dil, rows, LANES), cur_map)
        prev = pl.BlockSpec((None, None, dil, ATTN_BLOCK, LANES), prev_map)
        in_specs += [cur, cur, cur, prev, prev]
    tail = pl.BlockSpec((tq, d), lambda s, hp: (jnp.maximum((s - 1) * N_PAIRS + hp, 0), 0))
    in_specs += [pl.BlockSpec(bias.shape, lambda s, hp: (0, 0, 0)),
                 pl.BlockSpec((None, SPAN, LANES), lambda s, hp: (hp, jnp.minimum(s, n_spans - 1), 0)),
                 tail, tail, tail, pl.BlockSpec(memory_space=pl.ANY), pl.BlockSpec(memory_space=pl.ANY)]
    args = []
    for q, k, v in qkv:
        args += [q, k, v, k, v]
    return pl.pallas_call(
        functools.partial(_attn_out_kernel, n_spans=n_spans, spans_per_seq=spans),
        grid=(n_spans + 1, N_PAIRS),
        in_specs=in_specs,
        out_specs=tail,
        out_shape=jax.ShapeDtypeStruct((t, d), F32),
        scratch_shapes=[pltpu.VMEM((len(B_PATTERNS), SPAN, LANES), F32)] * 3
        + [pltpu.VMEM((2, N_PAIRS, SPAN, LANES), BF16), pltpu.VMEM(wob.shape, BF16), pltpu.VMEM(wout.shape, BF16)]
        + _weight_scratch(wout.shape),
        compiler_params=_params("arbitrary", "arbitrary"),
        name="attn_out",
    )(*args, bias, bg, gb, za, x2, wob, wout)


def _band_bias():
    qi = lax.broadcasted_iota(jnp.int32, (ATTN_BLOCK, 2 * ATTN_BLOCK), 0)
    kj = lax.broadcasted_iota(jnp.int32, (ATTN_BLOCK, 2 * ATTN_BLOCK), 1)
    dist = qi + ATTN_BLOCK - kj
    band = (dist >= 0) & (dist <= ATTN_BLOCK)
    first = band & (kj >= ATTN_BLOCK)
    return jnp.stack([jnp.where(first, 0.0, NEG), jnp.where(band, 0.0, NEG)]).astype(F32)


def kernel(x, norm_g, w_in, a_ws, a_bs, a_ln_g, a_ln_b, b_qn_g, b_kn_g, w_oa, w_ob, w_out):
    bsz, seq, d = x.shape
    depth = w_in.shape[0]
    a_width = w_oa.shape[1]
    npat = len(B_PATTERNS)
    assert w_in.shape[2] == 3 * a_width + 3 * npat * B_WIDTH + B_WIDTH + 2 * d
    assert all(w // dil == ATTN_BLOCK and SPAN % (ATTN_BLOCK * dil) == 0 for w, dil in B_PATTERNS)
    assert seq % SPAN == 0
    t = bsz * seq
    band_bias = _band_bias()
    x2 = x.reshape(t, d)
    for l in range(depth):
        sgu_bias = jnp.repeat(a_bs[l].T, a_width // A_GROUPS, axis=1)
        gq = jnp.tile(b_qn_g[l], (1, B_HEADS)).reshape(npat, 1, B_WIDTH)
        gk = jnp.tile(b_kn_g[l], (1, B_HEADS)).reshape(npat, 1, B_WIDTH)
        res = _fused_in(x2, norm_g[l].reshape(1, d), w_in[l], a_ws[l], sgu_bias,
                        a_ln_g[l].reshape(1, -1), a_ln_b[l].reshape(1, -1), w_oa[l],
                        gq, gk, bsz=bsz, seq=seq)
        za, gb, bg = res[:3]
        qkv = [res[3 + 3 * p:6 + 3 * p] for p in range(npat)]
        x2 = _attn_out(qkv, band_bias, bg, gb, za, x2, w_ob[l], w_out[l],
                       bsz=bsz, seq=seq)
    return x2.reshape(bsz, seq, d)
```

```python
import functools

import jax
import jax.numpy as jnp
from jax import lax
from jax.experimental import pallas as pl
from jax.experimental.pallas import tpu as pltpu

F32 = jnp.float32
BF16 = jnp.bfloat16

EPS = 1e-6
NEG = -1e30
CHUNK = 128
A_GROUPS = 4
B_PATTERNS = ((128, 1), (512, 4), (2048, 16))
B_HEADS = 8
B_HEAD_DIM = 64
B_WIDTH = B_HEADS * B_HEAD_DIM
LANES = 128
N_PAIRS = B_WIDTH // LANES
ATTN_BLOCK = 128
SPAN = 2048
UNITS = SPAN // ATTN_BLOCK
Q_SCALE = B_HEAD_DIM ** -0.5 * 1.4426950408889634

VMEM_LIMIT_BYTES = 60 * 1024 * 1024


def _params(*semantics):
    return pltpu.CompilerParams(dimension_semantics=semantics, vmem_limit_bytes=VMEM_LIMIT_BYTES)


def _resident(shape):
    return pl.BlockSpec(shape, lambda *_: (0,) * len(shape), pipeline_mode=pl.Buffered(1))


WEIGHT_SLOTS = 3
WEIGHT_CHUNK_BYTES = 3 << 19


def _weight_scratch(shape):
    rows, cols = shape
    chunk = 16
    while 2 * chunk * cols * 4 <= WEIGHT_CHUNK_BYTES and rows % (2 * chunk) == 0:
        chunk *= 2
    return [pltpu.VMEM((WEIGHT_SLOTS, chunk, cols), F32), pltpu.SemaphoreType.DMA((WEIGHT_SLOTS,))]


def _load_weight_bf16(src_hbm, dst_ref, stage_ref, sem, src_col=0, dst_col=0):
    slots, chunk, ncols = stage_ref.shape
    n = src_hbm.shape[0] // chunk

    def copy(c):
        return pltpu.make_async_copy(src_hbm.at[pl.ds(c * chunk, chunk), pl.ds(src_col, ncols)],
                                     stage_ref.at[c % slots], sem.at[c % slots])

    for c in range(min(slots - 1, n)):
        copy(c).start()
    for c in range(n):
        if c + slots - 1 < n:
            copy(c + slots - 1).start()
        copy(c).wait()
        dst_ref[c * chunk:(c + 1) * chunk, dst_col:dst_col + ncols] = stage_ref[c % slots].astype(BF16)


def _head_rms(t, gain):
    low = lax.broadcasted_iota(jnp.int32, (1, LANES), 1) < B_HEAD_DIM
    cols = []
    for cb in range(N_PAIRS):
        blk = t[:, cb * LANES:(cb + 1) * LANES]
        sq = blk * blk
        s0 = jnp.sum(jnp.where(low, sq, 0.0), axis=-1, keepdims=True)
        s1 = jnp.sum(jnp.where(low, 0.0, sq), axis=-1, keepdims=True)
        ms = jnp.where(low, s0, s1) * (1.0 / B_HEAD_DIM)
        cols.append(blk * lax.rsqrt(ms + EPS) * gain[:, cb * LANES:(cb + 1) * LANES])
    return jnp.concatenate(cols, axis=1)


def _emit_residue_major(val, out_ref, part, n_parts):
    dilation = out_ref.shape[1]
    rows = out_ref.shape[2] // n_parts
    for hp in range(N_PAIRS):
        for r in range(dilation):
            out_ref[hp, r, part * rows:(part + 1) * rows, :] = (
                val[r * rows:(r + 1) * rows, hp * LANES:(hp + 1) * LANES].astype(out_ref.dtype))


FUSED_IN_PARTS = 2


def _fused_in_kernel(x_ref, ng_ref, win_ref, ws_ref, sb_ref, lng_ref, lnb_ref, woa_ref, gq_ref, gk_ref,
                     ya_ref, hout_ref, bg_ref, q0_ref, k0_ref, v0_ref, q1_ref, k1_ref, v1_ref,
                     q2_ref, k2_ref, v2_ref, h_ref, h4_ref, h16_ref, xs_ref, s_ref, vb_ref, yp_ref,
                     win_bf_ref, woa_bf_ref, win_stage_ref, win_sem, woa_stage_ref, woa_sem):
    tm, d_model = x_ref.shape
    a_width = woa_ref.shape[0]
    n_parts = h_ref.shape[0]
    th = tm // n_parts

    @pl.when(pl.program_id(0) == 0)
    def _():
        _load_weight_bf16(win_ref, win_bf_ref, win_stage_ref, win_sem)
        _load_weight_bf16(woa_ref, woa_bf_ref, woa_stage_ref, woa_sem)

    gw = a_width // A_GROUPS
    cbw = 512
    col_u, col_v, col_g = 0, a_width, 2 * a_width
    col_qkv = 3 * a_width
    col_bg = col_qkv + 3 * len(B_PATTERNS) * B_WIDTH
    h_by_dilation = {1: h_ref, 4: h4_ref, 16: h16_ref}
    outs = ((q0_ref, k0_ref, v0_ref), (q1_ref, k1_ref, v1_ref), (q2_ref, k2_ref, v2_ref))
    row = lax.broadcasted_iota(jnp.int32, (CHUNK, CHUNK), 0)
    col = lax.broadcasted_iota(jnp.int32, (CHUNK, CHUNK), 1)

    def proj(pt, col0, width, src_ref=h_ref):
        return jnp.dot(src_ref[pt], win_bf_ref[:, col0:col0 + width], preferred_element_type=F32)

    def rows_of(pt):
        return slice(pt * th, (pt + 1) * th)

    def stage_norm(pt):
        x = x_ref[rows_of(pt), :]
        ms = jnp.mean(x * x, axis=-1, keepdims=True)
        xn = x * lax.rsqrt(ms + EPS) * ng_ref[...]
        h_ref[pt] = xn.astype(BF16)
        hout_ref[rows_of(pt), :] = xn.astype(hout_ref.dtype)
        for s in range(d_model // LANES):
            xs_ref[pt, s] = xn[:, s * LANES:(s + 1) * LANES]
        for dil, hp_ref in ((4, h4_ref), (16, h16_ref)):
            rows = th // dil
            for s in range(d_model // LANES):
                for r in range(dil):
                    hp_ref[pt, r * rows:(r + 1) * rows, s * LANES:(s + 1) * LANES] = (
                        xs_ref[pt, s, pl.ds(r, rows, stride=dil), :].astype(BF16))

    def stage_v(pt):
        for cb in range(a_width // cbw):
            s_ref[pt, :, cb * cbw:(cb + 1) * cbw] = jax.nn.gelu(proj(pt, col_v + cb * cbw, cbw))

    def stage_qkv(pt, p):
        q_ref, k_ref, v_ref = outs[p]
        hsrc = h_by_dilation[B_PATTERNS[p][1]]
        base = col_qkv + p * 3 * B_WIDTH
        _emit_residue_major(_head_rms(proj(pt, base, B_WIDTH, hsrc), gq_ref[p] * Q_SCALE), q_ref, pt, n_parts)
        _emit_residue_major(_head_rms(proj(pt, base + B_WIDTH, B_WIDTH, hsrc), gk_ref[p]), k_ref, pt, n_parts)
        _emit_residue_major(proj(pt, base + 2 * B_WIDTH, B_WIDTH, hsrc), v_ref, pt, n_parts)

    def stage_layernorm(pt):
        v = s_ref[pt]
        mu = jnp.mean(v, axis=-1, keepdims=True)
        vc = v - mu
        var = jnp.mean(vc * vc, axis=-1, keepdims=True)
        vb_ref[pt] = (vc * lax.rsqrt(var + EPS) * lng_ref[...] + lnb_ref[...]).astype(BF16)

    def stage_spatial(pt):
        for g in range(A_GROUPS):
            w = jnp.where(row >= col, ws_ref[g], 0.0).astype(BF16)
            gs = slice(g * gw, (g + 1) * gw)
            for c in range(th // CHUNK):
                rs = slice(c * CHUNK, (c + 1) * CHUNK)
                s_ref[pt, rs, gs] = (jnp.dot(w, vb_ref[pt, rs, gs], preferred_element_type=F32)
                                     + sb_ref[:, gs])

    def stage_bgate(pt):
        bg = jax.nn.silu(proj(pt, col_bg, B_WIDTH))
        for hp in range(N_PAIRS):
            bg_ref[hp, rows_of(pt), :] = bg[:, hp * LANES:(hp + 1) * LANES].astype(bg_ref.dtype)

    def stage_gates(pt):
        for cb in range(a_width // cbw):
            cs = slice(cb * cbw, (cb + 1) * cbw)
            u = jax.nn.gelu(proj(pt, col_u + cb * cbw, cbw))
            gate = jax.nn.silu(proj(pt, col_g + cb * cbw, cbw))
            yp_ref[pt, :, cs] = (u * s_ref[pt, :, cs] * gate).astype(BF16)

    def stage_out(pt):
        for cb in range(d_model // cbw):
            cs = slice(cb * cbw, (cb + 1) * cbw)
            ya = jnp.dot(yp_ref[pt], woa_bf_ref[:, cs], preferred_element_type=F32)
            ya_ref[rows_of(pt), cs] = ya.astype(ya_ref.dtype)

    stages = (stage_norm, stage_v, lambda pt: stage_qkv(pt, 0), stage_layernorm,
              lambda pt: stage_qkv(pt, 1), stage_spatial,
              lambda pt: (stage_qkv(pt, 2), stage_bgate(pt)), stage_gates, stage_out)
    for k in range(len(stages) + n_parts - 1):
        for pt in range(n_parts):
            if 0 <= k - pt < len(stages):
                stages[k - pt](pt)


def _fused_in(x2, ng, win, ws, sb, lng, lnb, woa, gq, gk, *, bsz, seq, tm=512):
    t, d = x2.shape
    a_width = woa.shape[0]
    tiles_per_seq = seq // tm
    parts, th = FUSED_IN_PARTS, tm // FUSED_IN_PARTS
    n_cols = win.shape[1] - 2 * d
    row = lambda width: pl.BlockSpec((tm, width), lambda i: (i, 0))
    hbm = pl.BlockSpec(memory_space=pl.ANY)
    out_specs = [row(d), row(d), pl.BlockSpec((N_PAIRS, tm, LANES), lambda i: (0, i, 0))]
    out_shape = [jax.ShapeDtypeStruct((t, d), BF16), jax.ShapeDtypeStruct((t, d), BF16),
                 jax.ShapeDtypeStruct((N_PAIRS, t, LANES), BF16)]
    for _, dil in B_PATTERNS:
        spec = pl.BlockSpec((None, N_PAIRS, dil, tm // dil, LANES),
                            lambda i: (i // tiles_per_seq, 0, 0, i % tiles_per_seq, 0))
        shape = jax.ShapeDtypeStruct((bsz, N_PAIRS, dil, seq // dil, LANES), BF16)
        out_specs += [spec] * 3
        out_shape += [shape] * 3
    return pl.pallas_call(
        _fused_in_kernel,
        grid=(t // tm,),
        in_specs=[row(d), _resident(ng.shape), hbm, _resident(ws.shape),
                  _resident(sb.shape), _resident(lng.shape), _resident(lnb.shape), hbm,
                  _resident(gq.shape), _resident(gk.shape)],
        out_specs=out_specs,
        out_shape=out_shape,
        scratch_shapes=[pltpu.VMEM((parts, th, d), BF16), pltpu.VMEM((parts, th, d), BF16),
                        pltpu.VMEM((parts, th, d), BF16), pltpu.VMEM((parts, d // LANES, th, LANES), F32),
                        pltpu.VMEM((parts, th, a_width), F32), pltpu.VMEM((parts, th, a_width), BF16),
                        pltpu.VMEM((parts, th, a_width), BF16),
                        pltpu.VMEM((win.shape[0], n_cols), BF16), pltpu.VMEM(woa.shape, BF16)]
        + _weight_scratch((win.shape[0], n_cols)) + _weight_scratch(woa.shape),
        compiler_params=_params("arbitrary"),
        name="fused_in",
    )(x2, ng, win, ws, sb, lng, lnb, woa, gq, gk)


def _attn_unit(q2, k2, v2, bias, low):
    n = ATTN_BLOCK
    v2e = jnp.concatenate([v2, jnp.ones_like(v2)], axis=1)
    zero = jnp.zeros_like(q2)
    qs = jnp.concatenate([jnp.where(low, q2, zero), jnp.where(low, zero, q2)], axis=0)
    s = lax.dot_general(qs, k2, (((1,), (1,)), ((), ())), preferred_element_type=F32)
    s = s + jnp.concatenate([bias, bias], axis=0)
    m = jnp.max(s, axis=-1, keepdims=True)
    e = jnp.exp2(s - m).astype(BF16)
    oe = jnp.dot(e, v2e, preferred_element_type=F32)
    mb = jnp.broadcast_to(m, (2 * n, LANES))
    return (jnp.where(low, oe[:n, :LANES], oe[n:, :LANES]),
            jnp.where(low, oe[:n, LANES:], oe[n:, LANES:]),
            jnp.where(low, mb[:n], mb[n:]))


def _attn_out_kernel(q0_ref, k0_ref, v0_ref, kp0_ref, vp0_ref, q1_ref, k1_ref, v1_ref, kp1_ref, vp1_ref,
                     q2_ref, k2_ref, v2_ref, kp2_ref, vp2_ref, bias_ref, bg_ref,
                     hq_ref, ya_ref, x_ref, wob_ref, wout_ref, win_ref, out_ref,
                     num_ref, den_ref, max_ref, yb_ref, wob_bf_ref, wout_bf_ref, wg_bf_ref, stage_ref, sem,
                     *, n_spans, spans_per_seq, col_ga):
    s = pl.program_id(0)
    hp = pl.program_id(1)
    tq = out_ref.shape[0]

    @pl.when(jnp.logical_and(s == 0, hp == 0))
    def _():
        yb_ref[...] = jnp.zeros_like(yb_ref)
        _load_weight_bf16(wob_ref, wob_bf_ref, stage_ref, sem)
        _load_weight_bf16(wout_ref, wout_bf_ref, stage_ref, sem)
        d_model = wout_ref.shape[0]
        _load_weight_bf16(win_ref, wg_bf_ref, stage_ref, sem, src_col=col_ga)
        _load_weight_bf16(win_ref, wg_bf_ref, stage_ref, sem, src_col=col_ga + d_model, dst_col=d_model)

    def out_projection(lo, n):
        src_rows = pl.ds(pl.multiple_of(hp * tq + lo, n), n)
        yb = jnp.concatenate([yb_ref[(s + 1) % 2, j, src_rows, :] for j in range(N_PAIRS)], axis=1)
        yb = jnp.dot(yb, wob_bf_ref[...], preferred_element_type=F32)
        rs = slice(lo, lo + n)
        d_model = out_ref.shape[1]
        gates = jax.nn.sigmoid(jnp.dot(hq_ref[rs, :], wg_bf_ref[...], preferred_element_type=F32))
        merged = gates[:, :d_model] * ya_ref[rs, :].astype(F32) + gates[:, d_model:] * yb
        out_ref[rs, :] = x_ref[rs, :] + jnp.dot(merged.astype(BF16), wout_bf_ref[...],
                                                 preferred_element_type=F32)

    @pl.when(s == n_spans)
    def _():
        out_projection(0, tq)

    @pl.when(s < n_spans)
    def _():
        first_span = s % spans_per_seq == 0
        low = lax.broadcasted_iota(jnp.int32, (1, LANES), 1) < B_HEAD_DIM
        pats = ((q0_ref, k0_ref, v0_ref, kp0_ref, vp0_ref),
                (q1_ref, k1_ref, v1_ref, kp1_ref, vp1_ref),
                (q2_ref, k2_ref, v2_ref, kp2_ref, vp2_ref))
        def unit(p, r, c):
            q_ref, k_ref, v_ref, kp_ref, vp_ref = pats[p]
            dil = B_PATTERNS[p][1]
            rows = slice(c * ATTN_BLOCK, (c + 1) * ATTN_BLOCK)
            if c == 0:
                k2 = jnp.concatenate([kp_ref[r], k_ref[r, rows, :]], axis=0)
                v2 = jnp.concatenate([vp_ref[r], v_ref[r, rows, :]], axis=0)
                bias = bias_ref[jnp.where(first_span, 0, 1)]
            else:
                both = slice((c - 1) * ATTN_BLOCK, (c + 1) * ATTN_BLOCK)
                k2, v2 = k_ref[r, both, :], v_ref[r, both, :]
                bias = bias_ref[1]
            num, den, mx = _attn_unit(q_ref[r, rows, :], k2, v2, bias, low)
            dst = pl.ds(c * (ATTN_BLOCK * dil) + r, ATTN_BLOCK, stride=dil)
            num_ref[p, dst, :] = num
            den_ref[p, dst, :] = den
            max_ref[p, dst, :] = mx

        def combine(rs):
            m0, m1, m2 = max_ref[0, rs], max_ref[1, rs], max_ref[2, rs]
            mm = jnp.maximum(jnp.maximum(m0, m1), m2)
            a0, a1, a2 = jnp.exp2(m0 - mm), jnp.exp2(m1 - mm), jnp.exp2(m2 - mm)
            num = a0 * num_ref[0, rs] + a1 * num_ref[1, rs] + a2 * num_ref[2, rs]
            den = a0 * den_ref[0, rs] + a1 * den_ref[1, rs] + a2 * den_ref[2, rs]
            yb_ref[s % 2, hp, rs, :] = (num / den * bg_ref[rs, :].astype(F32)).astype(yb_ref.dtype)

        p16, p4, p1 = 2, 1, 0
        assert [B_PATTERNS[p][1] for p in (p16, p4, p1)] == [16, 4, 1]
        parts = 2
        for part in range(parts):
            out_projection(part * (tq // parts), tq // parts)
            for r in range(part * (16 // parts), (part + 1) * (16 // parts)):
                unit(p16, r, 0)
        for c4 in range(UNITS // 4):
            for r in range(4):
                unit(p4, r, c4)
            for c in range(4 * c4, 4 * c4 + 4):
                unit(p1, 0, c)
                if c % 2 == 1:
                    combine(slice((c - 1) * ATTN_BLOCK, (c + 1) * ATTN_BLOCK))


def _attn_out(qkv, bias, bg, hq, ya, x2, wob, wout, win, *, bsz, seq):
    spans = seq // SPAN
    n_spans = bsz * spans
    t, d = x2.shape
    tq = SPAN // N_PAIRS

    def span_of(s):
        sc = jnp.minimum(s, n_spans - 1)
        return sc // spans, sc % spans

    in_specs = []
    for _, dil in B_PATTERNS:
        rows = SPAN // dil

        def cur_map(s, hp):
            b, c = span_of(s)
            return (b, hp, 0, c, 0)

        def prev_map(s, hp, rows=rows):
            b, c = span_of(s)
            return (b, hp, 0, jnp.maximum(c * (rows // ATTN_BLOCK) - 1, 0), 0)

        cur = pl.BlockSpec((None, None, dil, rows, LANES), cur_map)
        prev = pl.BlockSpec((None, None, dil, ATTN_BLOCK, LANES), prev_map)
        in_specs += [cur, cur, cur, prev, prev]
    tail = pl.BlockSpec((tq, d), lambda s, hp: (jnp.maximum((s - 1) * N_PAIRS + hp, 0), 0))
    in_specs += [pl.BlockSpec(bias.shape, lambda s, hp: (0, 0, 0)),
                 pl.BlockSpec((None, SPAN, LANES), lambda s, hp: (hp, jnp.minimum(s, n_spans - 1), 0)),
                 tail, tail, tail] + [pl.BlockSpec(memory_space=pl.ANY)] * 3
    args = []
    for q, k, v in qkv:
        args += [q, k, v, k, v]
    return pl.pallas_call(
        functools.partial(_attn_out_kernel, n_spans=n_spans, spans_per_seq=spans,
                          col_ga=win.shape[1] - 2 * d),
        grid=(n_spans + 1, N_PAIRS),
        in_specs=in_specs,
        out_specs=tail,
        out_shape=jax.ShapeDtypeStruct((t, d), F32),
        scratch_shapes=[pltpu.VMEM((len(B_PATTERNS), SPAN, LANES), F32)] * 3
        + [pltpu.VMEM((2, N_PAIRS, SPAN, LANES), BF16), pltpu.VMEM(wob.shape, BF16), pltpu.VMEM(wout.shape, BF16),
           pltpu.VMEM((d, 2 * d), BF16)]
        + _weight_scratch(wout.shape),
        compiler_params=_params("arbitrary", "arbitrary"),
        name="attn_out",
    )(*args, bias, bg, hq, ya, x2, wob, wout, win)


def _band_bias():
    qi = lax.broadcasted_iota(jnp.int32, (ATTN_BLOCK, 2 * ATTN_BLOCK), 0)
    kj = lax.broadcasted_iota(jnp.int32, (ATTN_BLOCK, 2 * ATTN_BLOCK), 1)
    dist = qi + ATTN_BLOCK - kj
    band = (dist >= 0) & (dist <= ATTN_BLOCK)
    first = band & (kj >= ATTN_BLOCK)
    return jnp.stack([jnp.where(first, 0.0, NEG), jnp.where(band, 0.0, NEG)]).astype(F32)


def kernel(x, norm_g, w_in, a_ws, a_bs, a_ln_g, a_ln_b, b_qn_g, b_kn_g, w_oa, w_ob, w_out):
    bsz, seq, d = x.shape
    depth = w_in.shape[0]
    a_width = w_oa.shape[1]
    npat = len(B_PATTERNS)
    assert w_in.shape[2] == 3 * a_width + 3 * npat * B_WIDTH + B_WIDTH + 2 * d
    assert all(w // dil == ATTN_BLOCK and SPAN % (ATTN_BLOCK * dil) == 0 for w, dil in B_PATTERNS)
    assert seq % SPAN == 0
    t = bsz * seq
    band_bias = _band_bias()
    x2 = x.reshape(t, d)
    for l in range(depth):
        sgu_bias = jnp.repeat(a_bs[l].T, a_width // A_GROUPS, axis=1)
        gq = jnp.tile(b_qn_g[l], (1, B_HEADS)).reshape(npat, 1, B_WIDTH)
        gk = jnp.tile(b_kn_g[l], (1, B_HEADS)).reshape(npat, 1, B_WIDTH)
        res = _fused_in(x2, norm_g[l].reshape(1, d), w_in[l], a_ws[l], sgu_bias,
                        a_ln_g[l].reshape(1, -1), a_ln_b[l].reshape(1, -1), w_oa[l],
                        gq, gk, bsz=bsz, seq=seq)
        ya, hq, bg = res[:3]
        qkv = [res[3 + 3 * p:6 + 3 * p] for p in range(npat)]
        x2 = _attn_out(qkv, band_bias, bg, hq, ya, x2, w_ob[l], w_out[l], w_in[l],
                       bsz=bsz, seq=seq)
    return x2.reshape(bsz, seq, d)
```

```python
import functools

import jax
import jax.numpy as jnp
from jax import lax
from jax.experimental import pallas as pl
from jax.experimental.pallas import tpu as pltpu

F32 = jnp.float32
BF16 = jnp.bfloat16

EPS = 1e-6
NEG = -1e30
CHUNK = 128
A_GROUPS = 4
B_PATTERNS = ((128, 1), (512, 4), (2048, 16))
B_HEADS = 8
B_HEAD_DIM = 64
B_WIDTH = B_HEADS * B_HEAD_DIM
LANES = 128
N_PAIRS = B_WIDTH // LANES
ATTN_BLOCK = 128
SPAN = 2048
UNITS = SPAN // ATTN_BLOCK
Q_ORDER = max(d for _, d in B_PATTERNS)
CLASS_ROWS = SPAN // Q_ORDER
Q_SCALE = B_HEAD_DIM ** -0.5 * 1.4426950408889634

VMEM_LIMIT_BYTES = 60 * 1024 * 1024


def _params(*semantics):
    return pltpu.CompilerParams(dimension_semantics=semantics, vmem_limit_bytes=VMEM_LIMIT_BYTES)


def _resident(shape):
    return pl.BlockSpec(shape, lambda *_: (0,) * len(shape), pipeline_mode=pl.Buffered(1))


WEIGHT_SLOTS = 3
WEIGHT_CHUNK_BYTES = 3 << 19


def _weight_scratch(shape):
    rows, cols = shape
    chunk = 16
    while 2 * chunk * cols * 4 <= WEIGHT_CHUNK_BYTES and rows % (2 * chunk) == 0:
        chunk *= 2
    return [pltpu.VMEM((WEIGHT_SLOTS, chunk, cols), F32), pltpu.SemaphoreType.DMA((WEIGHT_SLOTS,))]


def _load_weight_bf16(src_hbm, dst_ref, stage_ref, sem):
    slots, chunk, _ = stage_ref.shape
    n = src_hbm.shape[0] // chunk

    def copy(c):
        return pltpu.make_async_copy(src_hbm.at[pl.ds(c * chunk, chunk)], stage_ref.at[c % slots],
                                     sem.at[c % slots])

    for c in range(min(slots - 1, n)):
        copy(c).start()
    for c in range(n):
        if c + slots - 1 < n:
            copy(c + slots - 1).start()
        copy(c).wait()
        dst_ref[c * chunk:(c + 1) * chunk, :] = stage_ref[c % slots].astype(BF16)


def _head_rms(t, gain):
    low = lax.broadcasted_iota(jnp.int32, (1, LANES), 1) < B_HEAD_DIM
    cols = []
    for cb in range(N_PAIRS):
        blk = t[:, cb * LANES:(cb + 1) * LANES]
        sq = blk * blk
        s0 = jnp.sum(jnp.where(low, sq, 0.0), axis=-1, keepdims=True)
        s1 = jnp.sum(jnp.where(low, 0.0, sq), axis=-1, keepdims=True)
        ms = jnp.where(low, s0, s1) * (1.0 / B_HEAD_DIM)
        cols.append(blk * lax.rsqrt(ms + EPS) * gain[:, cb * LANES:(cb + 1) * LANES])
    return jnp.concatenate(cols, axis=1)


def _emit_residue_major(val, out_ref, part, n_parts):
    dilation = out_ref.shape[1]
    rows = out_ref.shape[2] // n_parts
    for hp in range(N_PAIRS):
        for r in range(dilation):
            out_ref[hp, r, part * rows:(part + 1) * rows, :] = (
                val[r * rows:(r + 1) * rows, hp * LANES:(hp + 1) * LANES].astype(out_ref.dtype))


FUSED_IN_PARTS = 2


def _fused_in_kernel(x_ref, ng_ref, win_ref, ws_ref, sb_ref, lng_ref, lnb_ref, woa_ref, gq_ref, gk_ref,
                     za_ref, gb_ref, bg_ref, q0_ref, k0_ref, v0_ref, q1_ref, k1_ref, v1_ref,
                     q2_ref, k2_ref, v2_ref, h_ref, h4_ref, h16_ref, xs_ref, s_ref, vb_ref, yp_ref,
                     win_bf_ref, woa_bf_ref, win_stage_ref, win_sem, woa_stage_ref, woa_sem):
    tm, d_model = x_ref.shape
    a_width = woa_ref.shape[0]
    n_parts = h_ref.shape[0]
    th = tm // n_parts

    @pl.when(pl.program_id(0) == 0)
    def _():
        _load_weight_bf16(win_ref, win_bf_ref, win_stage_ref, win_sem)
        _load_weight_bf16(woa_ref, woa_bf_ref, woa_stage_ref, woa_sem)

    gw = a_width // A_GROUPS
    cbw = 512
    col_u, col_v, col_g = 0, a_width, 2 * a_width
    col_qkv = 3 * a_width
    col_bg = col_qkv + 3 * len(B_PATTERNS) * B_WIDTH
    col_ga = col_bg + B_WIDTH
    col_gb = col_ga + d_model
    h_by_dilation = {1: h_ref, 4: h4_ref, 16: h16_ref}
    outs = ((q0_ref, k0_ref, v0_ref), (q1_ref, k1_ref, v1_ref), (q2_ref, k2_ref, v2_ref))
    row = lax.broadcasted_iota(jnp.int32, (CHUNK, CHUNK), 0)
    col = lax.broadcasted_iota(jnp.int32, (CHUNK, CHUNK), 1)

    def proj(pt, col0, width, src_ref=h_ref):
        return jnp.dot(src_ref[pt], win_bf_ref[:, col0:col0 + width], preferred_element_type=F32)

    def rows_of(pt):
        return slice(pt * th, (pt + 1) * th)

    def stage_norm(pt):
        x = x_ref[rows_of(pt), :]
        ms = jnp.mean(x * x, axis=-1, keepdims=True)
        xn = x * lax.rsqrt(ms + EPS) * ng_ref[...]
        h_ref[pt] = xn.astype(BF16)
        for s in range(d_model // LANES):
            xs_ref[pt, s] = xn[:, s * LANES:(s + 1) * LANES]
        for dil, hp_ref in ((4, h4_ref), (16, h16_ref)):
            rows = th // dil
            for s in range(d_model // LANES):
                for r in range(dil):
                    hp_ref[pt, r * rows:(r + 1) * rows, s * LANES:(s + 1) * LANES] = (
                        xs_ref[pt, s, pl.ds(r, rows, stride=dil), :].astype(BF16))

    def stage_v(pt):
        for cb in range(a_width // cbw):
            s_ref[pt, :, cb * cbw:(cb + 1) * cbw] = jax.nn.gelu(proj(pt, col_v + cb * cbw, cbw))

    def emit_q(q, q_ref, dil, pt):
        rows = th // Q_ORDER
        if dil == Q_ORDER:
            _emit_residue_major(q, q_ref, pt, n_parts)
        elif dil == 1:
            for hp in range(N_PAIRS):
                for a in range(Q_ORDER):
                    for blk in range(th // ATTN_BLOCK):
                        per_blk = ATTN_BLOCK // Q_ORDER
                        xs_ref[pt, hp, blk * ATTN_BLOCK + a * per_blk:blk * ATTN_BLOCK + (a + 1) * per_blk, :] = (
                            q[a * rows + blk * per_blk:a * rows + (blk + 1) * per_blk,
                              hp * LANES:(hp + 1) * LANES])
                q_ref[hp, 0, rows_of(pt), :] = xs_ref[pt, hp].astype(q_ref.dtype)
        else:
            assert tm == ATTN_BLOCK * dil
            per_tile = tm // Q_ORDER
            for hp in range(N_PAIRS):
                for cls in range(Q_ORDER):
                    dst = (cls // dil) * per_tile + pt * rows
                    q_ref[hp, cls % dil, dst:dst + rows, :] = (
                        q[cls * rows:(cls + 1) * rows, hp * LANES:(hp + 1) * LANES].astype(q_ref.dtype))

    def stage_qkv(pt, p):
        q_ref, k_ref, v_ref = outs[p]
        dil = B_PATTERNS[p][1]
        hsrc = h_by_dilation[dil]
        base = col_qkv + p * 3 * B_WIDTH
        emit_q(_head_rms(proj(pt, base, B_WIDTH, h_by_dilation[Q_ORDER]), gq_ref[p] * Q_SCALE), q_ref, dil, pt)
        _emit_residue_major(_head_rms(proj(pt, base + B_WIDTH, B_WIDTH, hsrc), gk_ref[p]), k_ref, pt, n_parts)
        _emit_residue_major(proj(pt, base + 2 * B_WIDTH, B_WIDTH, hsrc), v_ref, pt, n_parts)

    def stage_layernorm(pt):
        v = s_ref[pt]
        mu = jnp.mean(v, axis=-1, keepdims=True)
        vc = v - mu
        var = jnp.mean(vc * vc, axis=-1, keepdims=True)
        vb_ref[pt] = (vc * lax.rsqrt(var + EPS) * lng_ref[...] + lnb_ref[...]).astype(BF16)

    def stage_spatial(pt):
        for g in range(A_GROUPS):
            w = jnp.where(row >= col, ws_ref[g], 0.0).astype(BF16)
            gs = slice(g * gw, (g + 1) * gw)
            for c in range(th // CHUNK):
                rs = slice(c * CHUNK, (c + 1) * CHUNK)
                s_ref[pt, rs, gs] = (jnp.dot(w, vb_ref[pt, rs, gs], preferred_element_type=F32)
                                     + sb_ref[:, gs])

    def stage_bgate(pt):
        bg = jax.nn.silu(proj(pt, col_bg, B_WIDTH))
        for hp in range(N_PAIRS):
            bg_ref[hp, rows_of(pt), :] = bg[:, hp * LANES:(hp + 1) * LANES].astype(bg_ref.dtype)

    def stage_gates(pt):
        for cb in range(a_width // cbw):
            cs = slice(cb * cbw, (cb + 1) * cbw)
            u = jax.nn.gelu(proj(pt, col_u + cb * cbw, cbw))
            gate = jax.nn.silu(proj(pt, col_g + cb * cbw, cbw))
            yp_ref[pt, :, cs] = (u * s_ref[pt, :, cs] * gate).astype(BF16)

    def stage_out(pt):
        for cb in range(d_model // cbw):
            cs = slice(cb * cbw, (cb + 1) * cbw)
            gb_ref[rows_of(pt), cs] = jax.nn.sigmoid(proj(pt, col_gb + cb * cbw, cbw)).astype(gb_ref.dtype)
            ya = jnp.dot(yp_ref[pt], woa_bf_ref[:, cs], preferred_element_type=F32)
            za_ref[rows_of(pt), cs] = (jax.nn.sigmoid(proj(pt, col_ga + cb * cbw, cbw)) * ya).astype(za_ref.dtype)

    stages = (stage_norm, stage_v, lambda pt: stage_qkv(pt, 0), stage_layernorm,
              lambda pt: stage_qkv(pt, 1), stage_spatial,
              lambda pt: (stage_qkv(pt, 2), stage_bgate(pt)), stage_gates, stage_out)
    for k in range(len(stages) + n_parts - 1):
        for pt in range(n_parts):
            if 0 <= k - pt < len(stages):
                stages[k - pt](pt)


def _fused_in(x2, ng, win, ws, sb, lng, lnb, woa, gq, gk, *, bsz, seq, tm=512):
    t, d = x2.shape
    a_width = woa.shape[0]
    tiles_per_seq = seq // tm
    parts, th = FUSED_IN_PARTS, tm // FUSED_IN_PARTS
    row = lambda width: pl.BlockSpec((tm, width), lambda i: (i, 0))
    hbm = pl.BlockSpec(memory_space=pl.ANY)
    out_specs = [row(d), row(d), pl.BlockSpec((N_PAIRS, tm, LANES), lambda i: (0, i, 0))]
    out_shape = [jax.ShapeDtypeStruct((t, d), BF16), jax.ShapeDtypeStruct((t, d), BF16),
                 jax.ShapeDtypeStruct((N_PAIRS, t, LANES), BF16)]
    for _, dil in B_PATTERNS:
        spec = pl.BlockSpec((None, N_PAIRS, dil, tm // dil, LANES),
                            lambda i: (i // tiles_per_seq, 0, 0, i % tiles_per_seq, 0))
        shape = jax.ShapeDtypeStruct((bsz, N_PAIRS, dil, seq // dil, LANES), BF16)
        out_specs += [spec] * 3
        out_shape += [shape] * 3
    return pl.pallas_call(
        _fused_in_kernel,
        grid=(t // tm,),
        in_specs=[row(d), _resident(ng.shape), hbm, _resident(ws.shape),
                  _resident(sb.shape), _resident(lng.shape), _resident(lnb.shape), hbm,
                  _resident(gq.shape), _resident(gk.shape)],
        out_specs=out_specs,
        out_shape=out_shape,
        scratch_shapes=[pltpu.VMEM((parts, th, d), BF16), pltpu.VMEM((parts, th, d), BF16),
                        pltpu.VMEM((parts, th, d), BF16), pltpu.VMEM((parts, d // LANES, th, LANES), F32),
                        pltpu.VMEM((parts, th, a_width), F32), pltpu.VMEM((parts, th, a_width), BF16),
                        pltpu.VMEM((parts, th, a_width), BF16),
                        pltpu.VMEM(win.shape, BF16), pltpu.VMEM(woa.shape, BF16)]
        + _weight_scratch(win.shape) + _weight_scratch(woa.shape),
        compiler_params=_params("arbitrary"),
        name="fused_in",
    )(x2, ng, win, ws, sb, lng, lnb, woa, gq, gk)


def _attn_unit(q2, k2, v2, bias, low):
    n = ATTN_BLOCK
    v2e = jnp.concatenate([v2, jnp.ones_like(v2)], axis=1)
    zero = jnp.zeros_like(q2)
    qs = jnp.concatenate([jnp.where(low, q2, zero), jnp.where(low, zero, q2)], axis=0)
    s = lax.dot_general(qs, k2, (((1,), (1,)), ((), ())), preferred_element_type=F32)
    s = s + jnp.concatenate([bias, bias], axis=0)
    m = jnp.max(s, axis=-1, keepdims=True)
    e = jnp.exp2(s - m).astype(BF16)
    oe = jnp.dot(e, v2e, preferred_element_type=F32)
    mb = jnp.broadcast_to(m, (2 * n, LANES))
    return (jnp.where(low, oe[:n, :LANES], oe[n:, :LANES]),
            jnp.where(low, oe[:n, LANES:], oe[n:, LANES:]),
            jnp.where(low, mb[:n], mb[n:]))


def _attn_out_kernel(q0_ref, k0_ref, v0_ref, kp0_ref, vp0_ref, q1_ref, k1_ref, v1_ref, kp1_ref, vp1_ref,
                     q2_ref, k2_ref, v2_ref, kp2_ref, vp2_ref, bias_ref, bg_ref,
                     gb_ref, za_ref, x_ref, wob_ref, wout_ref, out_ref,
                     num_ref, den_ref, max_ref, ybn_ref, yb_ref, wob_bf_ref, wout_bf_ref, stage_ref, sem,
                     *, n_spans, spans_per_seq):
    s = pl.program_id(0)
    hp = pl.program_id(1)
    tq = out_ref.shape[0]

    @pl.when(jnp.logical_and(s == 0, hp == 0))
    def _():
        yb_ref[...] = jnp.zeros_like(yb_ref)
        _load_weight_bf16(wob_ref, wob_bf_ref, stage_ref, sem)
        _load_weight_bf16(wout_ref, wout_bf_ref, stage_ref, sem)

    def out_projection(lo, n):
        src_rows = pl.ds(pl.multiple_of(hp * tq + lo, n), n)
        yb = jnp.concatenate([yb_ref[(s + 1) % 2, j, src_rows, :] for j in range(N_PAIRS)], axis=1)
        yb = jnp.dot(yb, wob_bf_ref[...], preferred_element_type=F32)
        rs = slice(lo, lo + n)
        merged = za_ref[rs, :].astype(F32) + gb_ref[rs, :].astype(F32) * yb
        out_ref[rs, :] = x_ref[rs, :] + jnp.dot(merged.astype(BF16), wout_bf_ref[...],
                                                 preferred_element_type=F32)

    @pl.when(s == n_spans)
    def _():
        out_projection(0, tq)

    @pl.when(s < n_spans)
    def _():
        first_span = s % spans_per_seq == 0
        low = lax.broadcasted_iota(jnp.int32, (1, LANES), 1) < B_HEAD_DIM
        pats = ((q0_ref, k0_ref, v0_ref, kp0_ref, vp0_ref),
                (q1_ref, k1_ref, v1_ref, kp1_ref, vp1_ref),
                (q2_ref, k2_ref, v2_ref, kp2_ref, vp2_ref))
        def unit(p, r, c):
            q_ref, k_ref, v_ref, kp_ref, vp_ref = pats[p]
            dil = B_PATTERNS[p][1]
            rows = slice(c * ATTN_BLOCK, (c + 1) * ATTN_BLOCK)
            if c == 0:
                k2 = jnp.concatenate([kp_ref[r], k_ref[r, rows, :]], axis=0)
                v2 = jnp.concatenate([vp_ref[r], v_ref[r, rows, :]], axis=0)
                bias = bias_ref[p, jnp.where(first_span, 0, 1)]
            else:
                both = slice((c - 1) * ATTN_BLOCK, (c + 1) * ATTN_BLOCK)
                k2, v2 = k_ref[r, both, :], v_ref[r, both, :]
                bias = bias_ref[p, 1]
            tiles = _attn_unit(q_ref[r, rows, :], k2, v2, bias, low)
            groups = Q_ORDER // dil
            g_rows = ATTN_BLOCK // groups
            for g in range(groups):
                dst0 = (r + dil * g) * CLASS_ROWS + c * g_rows
                for ref, tile in zip((num_ref, den_ref, max_ref), tiles):
                    ref[p, dst0:dst0 + g_rows, :] = tile[g * g_rows:(g + 1) * g_rows]

        def combine(cls):
            rs = slice(cls * CLASS_ROWS, (cls + 1) * CLASS_ROWS)
            m0, m1, m2 = max_ref[0, rs], max_ref[1, rs], max_ref[2, rs]
            mm = jnp.maximum(jnp.maximum(m0, m1), m2)
            a0, a1, a2 = jnp.exp2(m0 - mm), jnp.exp2(m1 - mm), jnp.exp2(m2 - mm)
            num = a0 * num_ref[0, rs] + a1 * num_ref[1, rs] + a2 * num_ref[2, rs]
            den = a0 * den_ref[0, rs] + a1 * den_ref[1, rs] + a2 * den_ref[2, rs]
            ybn_ref[pl.ds(cls, CLASS_ROWS, stride=Q_ORDER), :] = num / den

        p16, p4, p1 = 2, 1, 0
        assert [B_PATTERNS[p][1] for p in (p16, p4, p1)] == [16, 4, 1]
        for r in range(16):
            unit(p16, r, 0)
        for c4 in range(UNITS // 4):
            for r in range(4):
                unit(p4, r, c4)
        for c in range(UNITS):
            unit(p1, 0, c)
        parts = 2
        for part in range(parts):
            out_projection(part * (tq // parts), tq // parts)
            for cls in range(part * (Q_ORDER // parts), (part + 1) * (Q_ORDER // parts)):
                combine(cls)
        rows = 256
        for i in range(SPAN // rows):
            rs = slice(i * rows, (i + 1) * rows)
            yb_ref[s % 2, hp, rs, :] = (ybn_ref[rs, :] * bg_ref[rs, :].astype(F32)).astype(yb_ref.dtype)


def _attn_out(qkv, bias, bg, gb, za, x2, wob, wout, *, bsz, seq):
    spans = seq // SPAN
    n_spans = bsz * spans
    t, d = x2.shape
    tq = SPAN // N_PAIRS

    def span_of(s):
        sc = jnp.minimum(s, n_spans - 1)
        return sc // spans, sc % spans

    in_specs = []
    for _, dil in B_PATTERNS:
        rows = SPAN // dil

        def cur_map(s, hp):
            b, c = span_of(s)
            return (b, hp, 0, c, 0)

        def prev_map(s, hp, rows=rows):
            b, c = span_of(s)
            return (b, hp, 0, jnp.maximum(c * (rows // ATTN_BLOCK) - 1, 0), 0)

        cur = pl.BlockSpec((None, None, dil, rows, LANES), cur_map)
        prev = pl.BlockSpec((None, None, dil, ATTN_BLOCK, LANES), prev_map)
        in_specs += [cur, cur, cur, prev, prev]
    tail = pl.BlockSpec((tq, d), lambda s, hp: (jnp.maximum((s - 1) * N_PAIRS + hp, 0), 0))
    in_specs += [pl.BlockSpec(bias.shape, lambda s, hp: (0,) * bias.ndim),
                 pl.BlockSpec((None, SPAN, LANES), lambda s, hp: (hp, jnp.minimum(s, n_spans - 1), 0)),
                 tail, tail, tail, pl.BlockSpec(memory_space=pl.ANY), pl.BlockSpec(memory_space=pl.ANY)]
    args = []
    for q, k, v in qkv:
        args += [q, k, v, k, v]
    return pl.pallas_call(
        functools.partial(_attn_out_kernel, n_spans=n_spans, spans_per_seq=spans),
        grid=(n_spans + 1, N_PAIRS),
        in_specs=in_specs,
        out_specs=tail,
        out_shape=jax.ShapeDtypeStruct((t, d), F32),
        scratch_shapes=[pltpu.VMEM((len(B_PATTERNS), SPAN, LANES), F32)] * 3
        + [pltpu.VMEM((SPAN, LANES), F32), pltpu.VMEM((2, N_PAIRS, SPAN, LANES), BF16), pltpu.VMEM(wob.shape, BF16), pltpu.VMEM(wout.shape, BF16)]
        + _weight_scratch(wout.shape),
        compiler_params=_params("arbitrary", "arbitrary"),
        name="attn_out",
    )(*args, bias, bg, gb, za, x2, wob, wout)


def _band_bias():
    kj = lax.broadcasted_iota(jnp.int32, (ATTN_BLOCK, 2 * ATTN_BLOCK), 1)
    row = lax.broadcasted_iota(jnp.int32, (ATTN_BLOCK, 2 * ATTN_BLOCK), 0)
    out = []
    for _, dil in B_PATTERNS:
        groups = Q_ORDER // dil
        g_rows = ATTN_BLOCK // groups
        qi = (row % g_rows) * groups + row // g_rows
        dist = qi + ATTN_BLOCK - kj
        band = (dist >= 0) & (dist <= ATTN_BLOCK)
        first = band & (kj >= ATTN_BLOCK)
        out.append(jnp.stack([jnp.where(first, 0.0, NEG), jnp.where(band, 0.0, NEG)]))
    return jnp.stack(out).astype(F32)


def kernel(x, norm_g, w_in, a_ws, a_bs, a_ln_g, a_ln_b, b_qn_g, b_kn_g, w_oa, w_ob, w_out):
    bsz, seq, d = x.shape
    depth = w_in.shape[0]
    a_width = w_oa.shape[1]
    npat = len(B_PATTERNS)
    assert w_in.shape[2] == 3 * a_width + 3 * npat * B_WIDTH + B_WIDTH + 2 * d
    assert all(w // dil == ATTN_BLOCK and SPAN % (ATTN_BLOCK * dil) == 0 for w, dil in B_PATTERNS)
    assert seq % SPAN == 0
    t = bsz * seq
    band_bias = _band_bias()
    x2 = x.reshape(t, d)
    for l in range(depth):
        sgu_bias = jnp.repeat(a_bs[l].T, a_width // A_GROUPS, axis=1)
        gq = jnp.tile(b_qn_g[l], (1, B_HEADS)).reshape(npat, 1, B_WIDTH)
        gk = jnp.tile(b_kn_g[l], (1, B_HEADS)).reshape(npat, 1, B_WIDTH)
        res = _fused_in(x2, norm_g[l].reshape(1, d), w_in[l], a_ws[l], sgu_bias,
                        a_ln_g[l].reshape(1, -1), a_ln_b[l].reshape(1, -1), w_oa[l],
                        gq, gk, bsz=bsz, seq=seq)
        za, gb, bg = res[:3]
        qkv = [res[3 + 3 * p:6 + 3 * p] for p in range(npat)]
        x2 = _attn_out(qkv, band_bias, bg, gb, za, x2, w_ob[l], w_out[l],
                       bsz=bsz, seq=seq)
    return x2.reshape(bsz, seq, d)
```

```python
import functools

import jax
import jax.numpy as jnp
from jax import lax
from jax.experimental import pallas as pl
from jax.experimental.pallas import tpu as pltpu

F32 = jnp.float32
BF16 = jnp.bfloat16

EPS = 1e-6
NEG = -1e30
CHUNK = 128
A_GROUPS = 4
B_PATTERNS = ((128, 1), (512, 4), (2048, 16))
B_HEADS = 8
B_HEAD_DIM = 64
B_WIDTH = B_HEADS * B_HEAD_DIM
LANES = 128
N_PAIRS = B_WIDTH // LANES
ATTN_BLOCK = 128
SPAN = 2048
UNITS = SPAN // ATTN_BLOCK
Q_ORDER = max(d for _, d in B_PATTERNS)
CLASS_ROWS = SPAN // Q_ORDER
Q_SCALE = B_HEAD_DIM ** -0.5 * 1.4426950408889634

VMEM_LIMIT_BYTES = 60 * 1024 * 1024


def _params(*semantics):
    return pltpu.CompilerParams(dimension_semantics=semantics, vmem_limit_bytes=VMEM_LIMIT_BYTES)


def _resident(shape):
    return pl.BlockSpec(shape, lambda *_: (0,) * len(shape), pipeline_mode=pl.Buffered(1))


WEIGHT_SLOTS = 3
WEIGHT_CHUNK_BYTES = 3 << 19


def _weight_scratch(shape):
    rows, cols = shape
    chunk = 16
    while 2 * chunk * cols * 4 <= WEIGHT_CHUNK_BYTES and rows % (2 * chunk) == 0:
        chunk *= 2
    return [pltpu.VMEM((WEIGHT_SLOTS, chunk, cols), F32), pltpu.SemaphoreType.DMA((WEIGHT_SLOTS,))]


def _load_weight_bf16(src_hbm, dst_ref, stage_ref, sem):
    slots, chunk, _ = stage_ref.shape
    n = src_hbm.shape[0] // chunk

    def copy(c):
        return pltpu.make_async_copy(src_hbm.at[pl.ds(c * chunk, chunk)], stage_ref.at[c % slots],
                                     sem.at[c % slots])

    for c in range(min(slots - 1, n)):
        copy(c).start()
    for c in range(n):
        if c + slots - 1 < n:
            copy(c + slots - 1).start()
        copy(c).wait()
        dst_ref[c * chunk:(c + 1) * chunk, :] = stage_ref[c % slots].astype(BF16)


def _head_rms(t, gain):
    low = lax.broadcasted_iota(jnp.int32, (1, LANES), 1) < B_HEAD_DIM
    cols = []
    for cb in range(N_PAIRS):
        blk = t[:, cb * LANES:(cb + 1) * LANES]
        sq = blk * blk
        s0 = jnp.sum(jnp.where(low, sq, 0.0), axis=-1, keepdims=True)
        s1 = jnp.sum(jnp.where(low, 0.0, sq), axis=-1, keepdims=True)
        ms = jnp.where(low, s0, s1) * (1.0 / B_HEAD_DIM)
        cols.append(blk * lax.rsqrt(ms + EPS) * gain[:, cb * LANES:(cb + 1) * LANES])
    return jnp.concatenate(cols, axis=1)


def _emit_residue_major(val, out_ref, part, n_parts):
    dilation = out_ref.shape[1]
    rows = out_ref.shape[2] // n_parts
    for hp in range(N_PAIRS):
        for r in range(dilation):
            out_ref[hp, r, part * rows:(part + 1) * rows, :] = (
                val[r * rows:(r + 1) * rows, hp * LANES:(hp + 1) * LANES].astype(out_ref.dtype))


FUSED_IN_PARTS = 2


def _fused_in_kernel(x_ref, ng_ref, win_ref, ws_ref, sb_ref, lng_ref, lnb_ref, woa_ref, gq_ref, gk_ref,
                     za_ref, gb_ref, bg_ref, q0_ref, k0_ref, v0_ref, q1_ref, k1_ref, v1_ref,
                     q2_ref, k2_ref, v2_ref, h_ref, h4_ref, h16_ref, xs_ref, s_ref, vb_ref, yp_ref,
                     win_bf_ref, woa_bf_ref, win_stage_ref, win_sem, woa_stage_ref, woa_sem):
    tm, d_model = x_ref.shape
    a_width = woa_ref.shape[0]
    n_parts = h_ref.shape[0]
    th = tm // n_parts

    @pl.when(pl.program_id(0) == 0)
    def _():
        _load_weight_bf16(win_ref, win_bf_ref, win_stage_ref, win_sem)
        _load_weight_bf16(woa_ref, woa_bf_ref, woa_stage_ref, woa_sem)

    gw = a_width // A_GROUPS
    cbw = 512
    col_u, col_v, col_g = 0, a_width, 2 * a_width
    col_qkv = 3 * a_width
    col_bg = col_qkv + 3 * len(B_PATTERNS) * B_WIDTH
    col_ga = col_bg + B_WIDTH
    col_gb = col_ga + d_model
    h_by_dilation = {1: h_ref, 4: h4_ref, 16: h16_ref}
    outs = ((q0_ref, k0_ref, v0_ref), (q1_ref, k1_ref, v1_ref), (q2_ref, k2_ref, v2_ref))
    row = lax.broadcasted_iota(jnp.int32, (CHUNK, CHUNK), 0)
    col = lax.broadcasted_iota(jnp.int32, (CHUNK, CHUNK), 1)

    def proj(pt, col0, width, src_ref=h_ref):
        return jnp.dot(src_ref[pt], win_bf_ref[:, col0:col0 + width], preferred_element_type=F32)

    def rows_of(pt):
        return slice(pt * th, (pt + 1) * th)

    def stage_norm(pt):
        x = x_ref[rows_of(pt), :]
        ms = jnp.mean(x * x, axis=-1, keepdims=True)
        xn = x * lax.rsqrt(ms + EPS) * ng_ref[...]
        h_ref[pt] = xn.astype(BF16)
        rows4, rows16 = th // 4, th // 16
        for s in range(d_model // LANES):
            lanes = slice(s * LANES, (s + 1) * LANES)
            xs_ref[0, s] = xn[:, lanes]
            for r in range(4):
                x4 = xs_ref[0, s, pl.ds(r, rows4, stride=4), :]
                h4_ref[pt, r * rows4:(r + 1) * rows4, lanes] = x4.astype(BF16)
                xs_ref[1, s, r * rows4:(r + 1) * rows4, :] = x4
            for cls in range(16):
                x16 = xs_ref[1, s, pl.ds((cls % 4) * rows4 + cls // 4, rows16, stride=4), :]
                h16_ref[pt, cls * rows16:(cls + 1) * rows16, lanes] = x16.astype(BF16)

    def stage_v(pt):
        for cb in range(a_width // cbw):
            s_ref[pt, :, cb * cbw:(cb + 1) * cbw] = jax.nn.gelu(proj(pt, col_v + cb * cbw, cbw))

    def emit_q(q, q_ref, dil, pt):
        rows = th // Q_ORDER
        if dil == Q_ORDER:
            _emit_residue_major(q, q_ref, pt, n_parts)
        elif dil == 1:
            per_blk = ATTN_BLOCK // Q_ORDER
            for hp in range(N_PAIRS):
                for blk in range(th // ATTN_BLOCK):
                    for a in range(0, Q_ORDER, 2):
                        pair = [q[c * rows + blk * per_blk:c * rows + (blk + 1) * per_blk,
                                  hp * LANES:(hp + 1) * LANES] for c in (a, a + 1)]
                        dst = pt * th + blk * ATTN_BLOCK + a * per_blk
                        q_ref[hp, 0, dst:dst + 2 * per_blk, :] = jnp.concatenate(pair, axis=0).astype(q_ref.dtype)
        else:
            assert tm == ATTN_BLOCK * dil
            per_tile = tm // Q_ORDER
            for hp in range(N_PAIRS):
                for cls in range(Q_ORDER):
                    dst = (cls // dil) * per_tile + pt * rows
                    q_ref[hp, cls % dil, dst:dst + rows, :] = (
                        q[cls * rows:(cls + 1) * rows, hp * LANES:(hp + 1) * LANES].astype(q_ref.dtype))

    def stage_qkv(pt, p):
        q_ref, k_ref, v_ref = outs[p]
        dil = B_PATTERNS[p][1]
        hsrc = h_by_dilation[dil]
        base = col_qkv + p * 3 * B_WIDTH
        emit_q(_head_rms(proj(pt, base, B_WIDTH, h_by_dilation[Q_ORDER]), gq_ref[p] * Q_SCALE), q_ref, dil, pt)
        _emit_residue_major(_head_rms(proj(pt, base + B_WIDTH, B_WIDTH, hsrc), gk_ref[p]), k_ref, pt, n_parts)
        _emit_residue_major(proj(pt, base + 2 * B_WIDTH, B_WIDTH, hsrc), v_ref, pt, n_parts)

    def stage_layernorm(pt):
        v = s_ref[pt]
        mu = jnp.mean(v, axis=-1, keepdims=True)
        vc = v - mu
        var = jnp.mean(vc * vc, axis=-1, keepdims=True)
        vb_ref[pt] = (vc * lax.rsqrt(var + EPS) * lng_ref[...] + lnb_ref[...]).astype(BF16)

    def stage_spatial(pt):
        for g in range(A_GROUPS):
            w = jnp.where(row >= col, ws_ref[g], 0.0).astype(BF16)
            gs = slice(g * gw, (g + 1) * gw)
            for c in range(th // CHUNK):
                rs = slice(c * CHUNK, (c + 1) * CHUNK)
                s_ref[pt, rs, gs] = (jnp.dot(w, vb_ref[pt, rs, gs], preferred_element_type=F32)
                                     + sb_ref[:, gs])

    def stage_bgate(pt):
        bg = jax.nn.silu(proj(pt, col_bg, B_WIDTH))
        for hp in range(N_PAIRS):
            bg_ref[hp, rows_of(pt), :] = bg[:, hp * LANES:(hp + 1) * LANES].astype(bg_ref.dtype)

    def stage_gates(pt):
        for cb in range(a_width // cbw):
            cs = slice(cb * cbw, (cb + 1) * cbw)
            u = jax.nn.gelu(proj(pt, col_u + cb * cbw, cbw))
            gate = jax.nn.silu(proj(pt, col_g + cb * cbw, cbw))
            yp_ref[pt, :, cs] = (u * s_ref[pt, :, cs] * gate).astype(BF16)

    def stage_out(pt):
        for cb in range(d_model // cbw):
            cs = slice(cb * cbw, (cb + 1) * cbw)
            gb_ref[rows_of(pt), cs] = jax.nn.sigmoid(proj(pt, col_gb + cb * cbw, cbw)).astype(gb_ref.dtype)
            ya = jnp.dot(yp_ref[pt], woa_bf_ref[:, cs], preferred_element_type=F32)
            za_ref[rows_of(pt), cs] = (jax.nn.sigmoid(proj(pt, col_ga + cb * cbw, cbw)) * ya).astype(za_ref.dtype)

    stages = (stage_norm, stage_v, lambda pt: stage_qkv(pt, 0), stage_layernorm,
              lambda pt: stage_qkv(pt, 1), stage_spatial,
              lambda pt: (stage_qkv(pt, 2), stage_bgate(pt)), stage_gates, stage_out)
    for k in range(len(stages) + n_parts - 1):
        for pt in range(n_parts):
            if 0 <= k - pt < len(stages):
                stages[k - pt](pt)


def _fused_in(x2, ng, win, ws, sb, lng, lnb, woa, gq, gk, *, bsz, seq, tm=512):
    t, d = x2.shape
    a_width = woa.shape[0]
    tiles_per_seq = seq // tm
    parts, th = FUSED_IN_PARTS, tm // FUSED_IN_PARTS
    row = lambda width: pl.BlockSpec((tm, width), lambda i: (i, 0))
    hbm = pl.BlockSpec(memory_space=pl.ANY)
    out_specs = [row(d), row(d), pl.BlockSpec((N_PAIRS, tm, LANES), lambda i: (0, i, 0))]
    out_shape = [jax.ShapeDtypeStruct((t, d), BF16), jax.ShapeDtypeStruct((t, d), BF16),
                 jax.ShapeDtypeStruct((N_PAIRS, t, LANES), BF16)]
    for _, dil in B_PATTERNS:
        spec = pl.BlockSpec((None, N_PAIRS, dil, tm // dil, LANES),
                            lambda i: (i // tiles_per_seq, 0, 0, i % tiles_per_seq, 0))
        shape = jax.ShapeDtypeStruct((bsz, N_PAIRS, dil, seq // dil, LANES), BF16)
        out_specs += [spec] * 3
        out_shape += [shape] * 3
    return pl.pallas_call(
        _fused_in_kernel,
        grid=(t // tm,),
        in_specs=[row(d), _resident(ng.shape), hbm, _resident(ws.shape),
                  _resident(sb.shape), _resident(lng.shape), _resident(lnb.shape), hbm,
                  _resident(gq.shape), _resident(gk.shape)],
        out_specs=out_specs,
        out_shape=out_shape,
        scratch_shapes=[pltpu.VMEM((parts, th, d), BF16), pltpu.VMEM((parts, th, d), BF16),
                        pltpu.VMEM((parts, th, d), BF16), pltpu.VMEM((2, d // LANES, th, LANES), F32),
                        pltpu.VMEM((parts, th, a_width), F32), pltpu.VMEM((parts, th, a_width), BF16),
                        pltpu.VMEM((parts, th, a_width), BF16),
                        pltpu.VMEM(win.shape, BF16), pltpu.VMEM(woa.shape, BF16)]
        + _weight_scratch(win.shape) + _weight_scratch(woa.shape),
        compiler_params=_params("arbitrary"),
        name="fused_in",
    )(x2, ng, win, ws, sb, lng, lnb, woa, gq, gk)


def _attn_unit(q2, k2, v2, bias, low):
    n = ATTN_BLOCK
    v2e = jnp.concatenate([v2, jnp.ones_like(v2)], axis=1)
    zero = jnp.zeros_like(q2)
    qs = jnp.concatenate([jnp.where(low, q2, zero), jnp.where(low, zero, q2)], axis=0)
    s = lax.dot_general(qs, k2, (((1,), (1,)), ((), ())), preferred_element_type=F32)
    s = s + jnp.concatenate([bias, bias], axis=0)
    m = jnp.max(s, axis=-1, keepdims=True)
    e = jnp.exp2(s - m).astype(BF16)
    oe = jnp.dot(e, v2e, preferred_element_type=F32)
    mb = jnp.broadcast_to(m, (2 * n, LANES))
    return (jnp.where(low, oe[:n, :LANES], oe[n:, :LANES]),
            jnp.where(low, oe[:n, LANES:], oe[n:, LANES:]),
            jnp.where(low, mb[:n], mb[n:]))


def _attn_out_kernel(q0_ref, k0_ref, v0_ref, kp0_ref, vp0_ref, q1_ref, k1_ref, v1_ref, kp1_ref, vp1_ref,
                     q2_ref, k2_ref, v2_ref, kp2_ref, vp2_ref, bias_ref, bg_ref,
                     gb_ref, za_ref, x_ref, wob_ref, wout_ref, out_ref,
                     num_ref, den_ref, max_ref, ybn_ref, yb_ref, wob_bf_ref, wout_bf_ref, stage_ref, sem,
                     *, n_spans, spans_per_seq):
    s = pl.program_id(0)
    hp = pl.program_id(1)
    tq = out_ref.shape[0]

    @pl.when(jnp.logical_and(s == 0, hp == 0))
    def _():
        yb_ref[...] = jnp.zeros_like(yb_ref)
        _load_weight_bf16(wob_ref, wob_bf_ref, stage_ref, sem)
        _load_weight_bf16(wout_ref, wout_bf_ref, stage_ref, sem)

    def out_projection(lo, n):
        src_rows = pl.ds(pl.multiple_of(hp * tq + lo, n), n)
        yb = jnp.concatenate([yb_ref[(s + 1) % 2, j, src_rows, :] for j in range(N_PAIRS)], axis=1)
        yb = jnp.dot(yb, wob_bf_ref[...], preferred_element_type=F32)
        rs = slice(lo, lo + n)
        merged = za_ref[rs, :].astype(F32) + gb_ref[rs, :].astype(F32) * yb
        out_ref[rs, :] = x_ref[rs, :] + jnp.dot(merged.astype(BF16), wout_bf_ref[...],
                                                 preferred_element_type=F32)

    @pl.when(s == n_spans)
    def _():
        out_projection(0, tq)

    @pl.when(s < n_spans)
    def _():
        first_span = s % spans_per_seq == 0
        low = lax.broadcasted_iota(jnp.int32, (1, LANES), 1) < B_HEAD_DIM
        pats = ((q0_ref, k0_ref, v0_ref, kp0_ref, vp0_ref),
                (q1_ref, k1_ref, v1_ref, kp1_ref, vp1_ref),
                (q2_ref, k2_ref, v2_ref, kp2_ref, vp2_ref))
        def unit(p, r, c):
            q_ref, k_ref, v_ref, kp_ref, vp_ref = pats[p]
            dil = B_PATTERNS[p][1]
            rows = slice(c * ATTN_BLOCK, (c + 1) * ATTN_BLOCK)
            if c == 0:
                k2 = jnp.concatenate([kp_ref[r], k_ref[r, rows, :]], axis=0)
                v2 = jnp.concatenate([vp_ref[r], v_ref[r, rows, :]], axis=0)
                bias = bias_ref[p, jnp.where(first_span, 0, 1)]
            else:
                both = slice((c - 1) * ATTN_BLOCK, (c + 1) * ATTN_BLOCK)
                k2, v2 = k_ref[r, both, :], v_ref[r, both, :]
                bias = bias_ref[p, 1]
            tiles = _attn_unit(q_ref[r, rows, :], k2, v2, bias, low)
            groups = Q_ORDER // dil
            g_rows = ATTN_BLOCK // groups
            for g in range(groups):
                dst0 = (r + dil * g) * CLASS_ROWS + c * g_rows
                for ref, tile in zip((num_ref, den_ref, max_ref), tiles):
                    ref[p, dst0:dst0 + g_rows, :] = tile[g * g_rows:(g + 1) * g_rows]

        def combine(cls):
            rs = slice(cls * CLASS_ROWS, (cls + 1) * CLASS_ROWS)
            m0, m1, m2 = max_ref[0, rs], max_ref[1, rs], max_ref[2, rs]
            mm = jnp.maximum(jnp.maximum(m0, m1), m2)
            a0, a1, a2 = jnp.exp2(m0 - mm), jnp.exp2(m1 - mm), jnp.exp2(m2 - mm)
            num = a0 * num_ref[0, rs] + a1 * num_ref[1, rs] + a2 * num_ref[2, rs]
            den = a0 * den_ref[0, rs] + a1 * den_ref[1, rs] + a2 * den_ref[2, rs]
            ybn_ref[pl.ds(cls, CLASS_ROWS, stride=Q_ORDER), :] = num / den

        p16, p4, p1 = 2, 1, 0
        assert [B_PATTERNS[p][1] for p in (p16, p4, p1)] == [16, 4, 1]
        for r in range(16):
            unit(p16, r, 0)
        for c4 in range(UNITS // 4):
            for r in range(4):
                unit(p4, r, c4)
        for c in range(UNITS):
            unit(p1, 0, c)
        parts = 2
        for part in range(parts):
            out_projection(part * (tq // parts), tq // parts)
            for cls in range(part * (Q_ORDER // parts), (part + 1) * (Q_ORDER // parts)):
                combine(cls)
        rows = 256
        for i in range(SPAN // rows):
            rs = slice(i * rows, (i + 1) * rows)
            yb_ref[s % 2, hp, rs, :] = (ybn_ref[rs, :] * bg_ref[rs, :].astype(F32)).astype(yb_ref.dtype)


def _attn_out(qkv, bias, bg, gb, za, x2, wob, wout, *, bsz, seq):
    spans = seq // SPAN
    n_spans = bsz * spans
    t, d = x2.shape
    tq = SPAN // N_PAIRS

    def span_of(s):
        sc = jnp.minimum(s, n_spans - 1)
        return sc // spans, sc % spans

    in_specs = []
    for _, dil in B_PATTERNS:
        rows = SPAN // dil

        def cur_map(s, hp):
            b, c = span_of(s)
            return (b, hp, 0, c, 0)

        def prev_map(s, hp, rows=rows):
            b, c = span_of(s)
            return (b, hp, 0, jnp.maximum(c * (rows // ATTN_BLOCK) - 1, 0), 0)

        cur = pl.BlockSpec((None, None, dil, rows, LANES), cur_map)
        prev = pl.BlockSpec((None, None, dil, ATTN_BLOCK, LANES), prev_map)
        in_specs += [cur, cur, cur, prev, prev]
    tail = pl.BlockSpec((tq, d), lambda s, hp: (jnp.maximum((s - 1) * N_PAIRS + hp, 0), 0))
    in_specs += [pl.BlockSpec(bias.shape, lambda s, hp: (0,) * bias.ndim),
                 pl.BlockSpec((None, SPAN, LANES), lambda s, hp: (hp, jnp.minimum(s, n_spans - 1), 0)),
                 tail, tail, tail, pl.BlockSpec(memory_space=pl.ANY), pl.BlockSpec(memory_space=pl.ANY)]
    args = []
    for q, k, v in qkv:
        args += [q, k, v, k, v]
    return pl.pallas_call(
        functools.partial(_attn_out_kernel, n_spans=n_spans, spans_per_seq=spans),
        grid=(n_spans + 1, N_PAIRS),
        in_specs=in_specs,
        out_specs=tail,
        out_shape=jax.ShapeDtypeStruct((t, d), F32),
        scratch_shapes=[pltpu.VMEM((len(B_PATTERNS), SPAN, LANES), F32)] * 3
        + [pltpu.VMEM((SPAN, LANES), F32), pltpu.VMEM((2, N_PAIRS, SPAN, LANES), BF16), pltpu.VMEM(wob.shape, BF16), pltpu.VMEM(wout.shape, BF16)]
        + _weight_scratch(wout.shape),
        compiler_params=_params("arbitrary", "arbitrary"),
        name="attn_out",
    )(*args, bias, bg, gb, za, x2, wob, wout)


def _band_bias():
    kj = lax.broadcasted_iota(jnp.int32, (ATTN_BLOCK, 2 * ATTN_BLOCK), 1)
    row = lax.broadcasted_iota(jnp.int32, (ATTN_BLOCK, 2 * ATTN_BLOCK), 0)
    out = []
    for _, dil in B_PATTERNS:
        groups = Q_ORDER // dil
        g_rows = ATTN_BLOCK // groups
        qi = (row % g_rows) * groups + row // g_rows
        dist = qi + ATTN_BLOCK - kj
        band = (dist >= 0) & (dist <= ATTN_BLOCK)
        first = band & (kj >= ATTN_BLOCK)
        out.append(jnp.stack([jnp.where(first, 0.0, NEG), jnp.where(band, 0.0, NEG)]))
    return jnp.stack(out).astype(F32)


def kernel(x, norm_g, w_in, a_ws, a_bs, a_ln_g, a_ln_b, b_qn_g, b_kn_g, w_oa, w_ob, w_out):
    bsz, seq, d = x.shape
    depth = w_in.shape[0]
    a_width = w_oa.shape[1]
    npat = len(B_PATTERNS)
    assert w_in.shape[2] == 3 * a_width + 3 * npat * B_WIDTH + B_WIDTH + 2 * d
    assert all(w // dil == ATTN_BLOCK and SPAN % (ATTN_BLOCK * dil) == 0 for w, dil in B_PATTERNS)
    assert seq % SPAN == 0
    t = bsz * seq
    band_bias = _band_bias()
    x2 = x.reshape(t, d)
    for l in range(depth):
        sgu_bias = jnp.repeat(a_bs[l].T, a_width // A_GROUPS, axis=1)
        gq = jnp.tile(b_qn_g[l], (1, B_HEADS)).reshape(npat, 1, B_WIDTH)
        gk = jnp.tile(b_kn_g[l], (1, B_HEADS)).reshape(npat, 1, B_WIDTH)
        res = _fused_in(x2, norm_g[l].reshape(1, d), w_in[l], a_ws[l], sgu_bias,
                        a_ln_g[l].reshape(1, -1), a_ln_b[l].reshape(1, -1), w_oa[l],
                        gq, gk, bsz=bsz, seq=seq)
        za, gb, bg = res[:3]
        qkv = [res[3 + 3 * p:6 + 3 * p] for p in range(npat)]
        x2 = _attn_out(qkv, band_bias, bg, gb, za, x2, w_ob[l], w_out[l],
                       bsz=bsz, seq=seq)
    return x2.reshape(bsz, seq, d)
```

```python
import functools

import jax
import jax.numpy as jnp
from jax import lax
from jax.experimental import pallas as pl
from jax.experimental.pallas import tpu as pltpu

F32 = jnp.float32
BF16 = jnp.bfloat16

EPS = 1e-6
NEG = -1e30
CHUNK = 128
A_GROUPS = 4
B_PATTERNS = ((128, 1), (512, 4), (2048, 16))
B_HEADS = 8
B_HEAD_DIM = 64
B_WIDTH = B_HEADS * B_HEAD_DIM
LANES = 128
N_PAIRS = B_WIDTH // LANES
ATTN_BLOCK = 128
SPAN = 2048
UNITS = SPAN // ATTN_BLOCK
Q_ORDER = max(d for _, d in B_PATTERNS)
CLASS_ROWS = SPAN // Q_ORDER
Q_SCALE = B_HEAD_DIM ** -0.5 * 1.4426950408889634

VMEM_LIMIT_BYTES = 60 * 1024 * 1024


def _params(*semantics):
    return pltpu.CompilerParams(dimension_semantics=semantics, vmem_limit_bytes=VMEM_LIMIT_BYTES)


def _resident(shape):
    return pl.BlockSpec(shape, lambda *_: (0,) * len(shape), pipeline_mode=pl.Buffered(1))


WEIGHT_SLOTS = 3
WEIGHT_CHUNK_BYTES = 3 << 19


def _weight_scratch(shape):
    rows, cols = shape
    chunk = 16
    while 2 * chunk * cols * 4 <= WEIGHT_CHUNK_BYTES and rows % (2 * chunk) == 0:
        chunk *= 2
    return [pltpu.VMEM((WEIGHT_SLOTS, chunk, cols), F32), pltpu.SemaphoreType.DMA((WEIGHT_SLOTS,))]


def _load_weight_bf16(src_hbm, dst_ref, stage_ref, sem):
    slots, chunk, _ = stage_ref.shape
    n = src_hbm.shape[0] // chunk

    def copy(c):
        return pltpu.make_async_copy(src_hbm.at[pl.ds(c * chunk, chunk)], stage_ref.at[c % slots],
                                     sem.at[c % slots])

    for c in range(min(slots - 1, n)):
        copy(c).start()
    for c in range(n):
        if c + slots - 1 < n:
            copy(c + slots - 1).start()
        copy(c).wait()
        dst_ref[c * chunk:(c + 1) * chunk, :] = stage_ref[c % slots].astype(BF16)


def _head_rms(t, gain):
    low = lax.broadcasted_iota(jnp.int32, (1, LANES), 1) < B_HEAD_DIM
    cols = []
    for cb in range(N_PAIRS):
        blk = t[:, cb * LANES:(cb + 1) * LANES]
        sq = blk * blk
        s0 = jnp.sum(jnp.where(low, sq, 0.0), axis=-1, keepdims=True)
        s1 = jnp.sum(jnp.where(low, 0.0, sq), axis=-1, keepdims=True)
        ms = jnp.where(low, s0, s1) * (1.0 / B_HEAD_DIM)
        cols.append(blk * lax.rsqrt(ms + EPS) * gain[:, cb * LANES:(cb + 1) * LANES])
    return jnp.concatenate(cols, axis=1)


def _emit_residue_major(val, out_ref, part, n_parts):
    dilation = out_ref.shape[1]
    rows = out_ref.shape[2] // n_parts
    for hp in range(N_PAIRS):
        for r in range(dilation):
            out_ref[hp, r, part * rows:(part + 1) * rows, :] = (
                val[r * rows:(r + 1) * rows, hp * LANES:(hp + 1) * LANES].astype(out_ref.dtype))


FUSED_IN_PARTS = 2


def _fused_in_kernel(x_ref, ng_ref, win_ref, ws_ref, sb_ref, lng_ref, lnb_ref, woa_ref, gq_ref, gk_ref,
                     za_ref, gb_ref, bg_ref, q0_ref, k0_ref, v0_ref, q1_ref, k1_ref, v1_ref,
                     q2_ref, k2_ref, v2_ref, h_ref, h4_ref, h16_ref, xs_ref, s_ref, vb_ref, yp_ref,
                     win_bf_ref, woa_bf_ref, win_stage_ref, win_sem, woa_stage_ref, woa_sem):
    tm, d_model = x_ref.shape
    a_width = woa_ref.shape[0]
    n_parts = h_ref.shape[0]
    th = tm // n_parts

    @pl.when(pl.program_id(0) == 0)
    def _():
        _load_weight_bf16(win_ref, win_bf_ref, win_stage_ref, win_sem)
        _load_weight_bf16(woa_ref, woa_bf_ref, woa_stage_ref, woa_sem)

    gw = a_width // A_GROUPS
    cbw = 256
    col_u, col_v, col_g = 0, a_width, 2 * a_width
    col_qkv = 3 * a_width
    col_bg = col_qkv + 3 * len(B_PATTERNS) * B_WIDTH
    col_ga = col_bg + B_WIDTH
    col_gb = col_ga + d_model
    h_by_dilation = {1: h_ref, 4: h4_ref, 16: h16_ref}
    outs = ((q0_ref, k0_ref, v0_ref), (q1_ref, k1_ref, v1_ref), (q2_ref, k2_ref, v2_ref))
    row = lax.broadcasted_iota(jnp.int32, (CHUNK, CHUNK), 0)
    col = lax.broadcasted_iota(jnp.int32, (CHUNK, CHUNK), 1)

    def proj(pt, col0, width, src_ref=h_ref):
        return jnp.dot(src_ref[pt], win_bf_ref[:, col0:col0 + width], preferred_element_type=F32)

    def rows_of(pt):
        return slice(pt * th, (pt + 1) * th)

    def stage_norm(pt):
        x = x_ref[rows_of(pt), :]
        ms = jnp.mean(x * x, axis=-1, keepdims=True)
        xn = x * lax.rsqrt(ms + EPS) * ng_ref[...]
        h_ref[pt] = xn.astype(BF16)
        rows4, rows16 = th // 4, th // 16
        for s in range(d_model // LANES):
            lanes = slice(s * LANES, (s + 1) * LANES)
            xs_ref[0, s] = xn[:, lanes]
            for r in range(4):
                x4 = xs_ref[0, s, pl.ds(r, rows4, stride=4), :]
                h4_ref[pt, r * rows4:(r + 1) * rows4, lanes] = x4.astype(BF16)
                xs_ref[1, s, r * rows4:(r + 1) * rows4, :] = x4
            for cls in range(16):
                x16 = xs_ref[1, s, pl.ds((cls % 4) * rows4 + cls // 4, rows16, stride=4), :]
                h16_ref[pt, cls * rows16:(cls + 1) * rows16, lanes] = x16.astype(BF16)

    def stage_v(pt):
        for cb in range(a_width // cbw):
            s_ref[pt, :, cb * cbw:(cb + 1) * cbw] = jax.nn.gelu(proj(pt, col_v + cb * cbw, cbw))

    def emit_q(q, q_ref, dil, pt):
        rows = th // Q_ORDER
        if dil == Q_ORDER:
            _emit_residue_major(q, q_ref, pt, n_parts)
        elif dil == 1:
            per_blk = ATTN_BLOCK // Q_ORDER
            for hp in range(N_PAIRS):
                for blk in range(th // ATTN_BLOCK):
                    for a in range(0, Q_ORDER, 2):
                        pair = [q[c * rows + blk * per_blk:c * rows + (blk + 1) * per_blk,
                                  hp * LANES:(hp + 1) * LANES] for c in (a, a + 1)]
                        dst = pt * th + blk * ATTN_BLOCK + a * per_blk
                        q_ref[hp, 0, dst:dst + 2 * per_blk, :] = jnp.concatenate(pair, axis=0).astype(q_ref.dtype)
        else:
            assert tm == ATTN_BLOCK * dil
            per_tile = tm // Q_ORDER
            for hp in range(N_PAIRS):
                for cls in range(Q_ORDER):
                    dst = (cls // dil) * per_tile + pt * rows
                    q_ref[hp, cls % dil, dst:dst + rows, :] = (
                        q[cls * rows:(cls + 1) * rows, hp * LANES:(hp + 1) * LANES].astype(q_ref.dtype))

    def stage_qkv(pt, p):
        q_ref, k_ref, v_ref = outs[p]
        dil = B_PATTERNS[p][1]
        hsrc = h_by_dilation[dil]
        base = col_qkv + p * 3 * B_WIDTH
        emit_q(_head_rms(proj(pt, base, B_WIDTH, h_by_dilation[Q_ORDER]), gq_ref[p] * Q_SCALE), q_ref, dil, pt)
        _emit_residue_major(_head_rms(proj(pt, base + B_WIDTH, B_WIDTH, hsrc), gk_ref[p]), k_ref, pt, n_parts)
        _emit_residue_major(proj(pt, base + 2 * B_WIDTH, B_WIDTH, hsrc), v_ref, pt, n_parts)

    def stage_layernorm(pt):
        v = s_ref[pt]
        mu = jnp.mean(v, axis=-1, keepdims=True)
        vc = v - mu
        var = jnp.mean(vc * vc, axis=-1, keepdims=True)
        vb_ref[pt] = (vc * lax.rsqrt(var + EPS) * lng_ref[...] + lnb_ref[...]).astype(BF16)

    def stage_spatial(pt):
        for g in range(A_GROUPS):
            w = jnp.where(row >= col, ws_ref[g], 0.0).astype(BF16)
            gs = slice(g * gw, (g + 1) * gw)
            for c in range(th // CHUNK):
                rs = slice(c * CHUNK, (c + 1) * CHUNK)
                s_ref[pt, rs, gs] = (jnp.dot(w, vb_ref[pt, rs, gs], preferred_element_type=F32)
                                     + sb_ref[:, gs])

    def stage_bgate(pt):
        bg = jax.nn.silu(proj(pt, col_bg, B_WIDTH))
        for hp in range(N_PAIRS):
            bg_ref[hp, rows_of(pt), :] = bg[:, hp * LANES:(hp + 1) * LANES].astype(bg_ref.dtype)

    def stage_gates(pt):
        for cb in range(a_width // cbw):
            cs = slice(cb * cbw, (cb + 1) * cbw)
            u = jax.nn.gelu(proj(pt, col_u + cb * cbw, cbw))
            gate = jax.nn.silu(proj(pt, col_g + cb * cbw, cbw))
            yp_ref[pt, :, cs] = (u * s_ref[pt, :, cs] * gate).astype(BF16)

    def stage_out(pt):
        for cb in range(d_model // cbw):
            cs = slice(cb * cbw, (cb + 1) * cbw)
            gb_ref[rows_of(pt), cs] = jax.nn.sigmoid(proj(pt, col_gb + cb * cbw, cbw)).astype(gb_ref.dtype)
            ya = jnp.dot(yp_ref[pt], woa_bf_ref[:, cs], preferred_element_type=F32)
            za_ref[rows_of(pt), cs] = (jax.nn.sigmoid(proj(pt, col_ga + cb * cbw, cbw)) * ya).astype(za_ref.dtype)

    stages = (stage_norm, stage_v, lambda pt: stage_qkv(pt, 0), stage_layernorm,
              lambda pt: stage_qkv(pt, 1), stage_spatial,
              lambda pt: (stage_qkv(pt, 2), stage_bgate(pt)), stage_gates, stage_out)
    for k in range(len(stages) + n_parts - 1):
        for pt in range(n_parts):
            if 0 <= k - pt < len(stages):
                stages[k - pt](pt)


def _fused_in(x2, ng, win, ws, sb, lng, lnb, woa, gq, gk, *, tm=512):
    t, d = x2.shape
    a_width = woa.shape[0]
    parts, th = FUSED_IN_PARTS, tm // FUSED_IN_PARTS
    row = lambda width: pl.BlockSpec((tm, width), lambda i: (i, 0))
    hbm = pl.BlockSpec(memory_space=pl.ANY)
    out_specs = [row(d), row(d), pl.BlockSpec((N_PAIRS, tm, LANES), lambda i: (0, i, 0))]
    out_shape = [jax.ShapeDtypeStruct((t, d), BF16), jax.ShapeDtypeStruct((t, d), BF16),
                 jax.ShapeDtypeStruct((N_PAIRS, t, LANES), BF16)]
    for _, dil in B_PATTERNS:
        spec = pl.BlockSpec((N_PAIRS, dil, tm // dil, LANES), lambda i: (0, 0, i, 0))
        shape = jax.ShapeDtypeStruct((N_PAIRS, dil, t // dil, LANES), BF16)
        out_specs += [spec] * 3
        out_shape += [shape] * 3
    return pl.pallas_call(
        _fused_in_kernel,
        grid=(t // tm,),
        in_specs=[row(d), _resident(ng.shape), hbm, _resident(ws.shape),
                  _resident(sb.shape), _resident(lng.shape), _resident(lnb.shape), hbm,
                  _resident(gq.shape), _resident(gk.shape)],
        out_specs=out_specs,
        out_shape=out_shape,
        scratch_shapes=[pltpu.VMEM((parts, th, d), BF16), pltpu.VMEM((parts, th, d), BF16),
                        pltpu.VMEM((parts, th, d), BF16), pltpu.VMEM((2, d // LANES, th, LANES), F32),
                        pltpu.VMEM((parts, th, a_width), F32), pltpu.VMEM((parts, th, a_width), BF16),
                        pltpu.VMEM((parts, th, a_width), BF16),
                        pltpu.VMEM(win.shape, BF16), pltpu.VMEM(woa.shape, BF16)]
        + _weight_scratch(win.shape) + _weight_scratch(woa.shape),
        compiler_params=_params("arbitrary"),
        name="fused_in",
    )(x2, ng, win, ws, sb, lng, lnb, woa, gq, gk)


def _attn_unit(q2, k2, v2, bias, low):
    n = ATTN_BLOCK
    v2e = jnp.concatenate([v2, jnp.ones_like(v2)], axis=1)
    zero = jnp.zeros_like(q2)
    qs = jnp.concatenate([jnp.where(low, q2, zero), jnp.where(low, zero, q2)], axis=0)
    s = lax.dot_general(qs, k2, (((1,), (1,)), ((), ())), preferred_element_type=F32)
    s = s + jnp.concatenate([bias, bias], axis=0)
    m = jnp.max(s, axis=-1, keepdims=True)
    e = jnp.exp2(s - m).astype(BF16)
    oe = jnp.dot(e, v2e, preferred_element_type=F32)
    mb = jnp.broadcast_to(m, (2 * n, LANES))
    return (jnp.where(low, oe[:n, :LANES], oe[n:, :LANES]),
            jnp.where(low, oe[:n, LANES:], oe[n:, LANES:]),
            jnp.where(low, mb[:n], mb[n:]))


def _attn_out_kernel(q0_ref, k0_ref, v0_ref, kp0_ref, vp0_ref, q1_ref, k1_ref, v1_ref, kp1_ref, vp1_ref,
                     q2_ref, k2_ref, v2_ref, kp2_ref, vp2_ref, bias_ref, bg_ref,
                     gb_ref, za_ref, x_ref, wob_ref, wout_ref, out_ref,
                     num_ref, den_ref, max_ref, ybn_ref, yb_ref, wob_bf_ref, wout_bf_ref, stage_ref, sem,
                     *, n_spans, spans_per_seq):
    s = pl.program_id(0)
    hp = pl.program_id(1)
    tq = out_ref.shape[0]

    @pl.when(jnp.logical_and(s == 0, hp == 0))
    def _():
        yb_ref[...] = jnp.zeros_like(yb_ref)
        _load_weight_bf16(wob_ref, wob_bf_ref, stage_ref, sem)
        _load_weight_bf16(wout_ref, wout_bf_ref, stage_ref, sem)

    def out_projection(lo, n):
        src_rows = pl.ds(pl.multiple_of(hp * tq + lo, n), n)
        yb = jnp.concatenate([yb_ref[(s + 1) % 2, j, src_rows, :] for j in range(N_PAIRS)], axis=1)
        yb = jnp.dot(yb, wob_bf_ref[...], preferred_element_type=F32)
        rs = slice(lo, lo + n)
        merged = za_ref[rs, :].astype(F32) + gb_ref[rs, :].astype(F32) * yb
        out_ref[rs, :] = x_ref[rs, :] + jnp.dot(merged.astype(BF16), wout_bf_ref[...],
                                                 preferred_element_type=F32)

    @pl.when(s == n_spans)
    def _():
        out_projection(0, tq)

    @pl.when(s < n_spans)
    def _():
        first_span = s % spans_per_seq == 0
        low = lax.broadcasted_iota(jnp.int32, (1, LANES), 1) < B_HEAD_DIM
        pats = ((q0_ref, k0_ref, v0_ref, kp0_ref, vp0_ref),
                (q1_ref, k1_ref, v1_ref, kp1_ref, vp1_ref),
                (q2_ref, k2_ref, v2_ref, kp2_ref, vp2_ref))
        def unit(p, r, c):
            q_ref, k_ref, v_ref, kp_ref, vp_ref = pats[p]
            dil = B_PATTERNS[p][1]
            rows = slice(c * ATTN_BLOCK, (c + 1) * ATTN_BLOCK)
            if c == 0:
                k2 = jnp.concatenate([kp_ref[r], k_ref[r, rows, :]], axis=0)
                v2 = jnp.concatenate([vp_ref[r], v_ref[r, rows, :]], axis=0)
                bias = bias_ref[p, jnp.where(first_span, 0, 1)]
            else:
                both = slice((c - 1) * ATTN_BLOCK, (c + 1) * ATTN_BLOCK)
                k2, v2 = k_ref[r, both, :], v_ref[r, both, :]
                bias = bias_ref[p, 1]
            tiles = _attn_unit(q_ref[r, rows, :], k2, v2, bias, low)
            groups = Q_ORDER // dil
            g_rows = ATTN_BLOCK // groups
            for g in range(groups):
                dst0 = (r + dil * g) * CLASS_ROWS + c * g_rows
                for ref, tile in zip((num_ref, den_ref, max_ref), tiles):
                    ref[p, dst0:dst0 + g_rows, :] = tile[g * g_rows:(g + 1) * g_rows]

        def combine(cls):
            rs = slice(cls * CLASS_ROWS, (cls + 1) * CLASS_ROWS)
            m0, m1, m2 = max_ref[0, rs], max_ref[1, rs], max_ref[2, rs]
            mm = jnp.maximum(jnp.maximum(m0, m1), m2)
            a0, a1, a2 = jnp.exp2(m0 - mm), jnp.exp2(m1 - mm), jnp.exp2(m2 - mm)
            num = a0 * num_ref[0, rs] + a1 * num_ref[1, rs] + a2 * num_ref[2, rs]
            den = a0 * den_ref[0, rs] + a1 * den_ref[1, rs] + a2 * den_ref[2, rs]
            ybn_ref[pl.ds(cls, CLASS_ROWS, stride=Q_ORDER), :] = num / den

        p16, p4, p1 = 2, 1, 0
        assert [B_PATTERNS[p][1] for p in (p16, p4, p1)] == [16, 4, 1]
        for r in range(16):
            unit(p16, r, 0)
        for c4 in range(UNITS // 4):
            for r in range(4):
                unit(p4, r, c4)
        for c in range(UNITS):
            unit(p1, 0, c)
        parts = 2
        for part in range(parts):
            out_projection(part * (tq // parts), tq // parts)
            for cls in range(part * (Q_ORDER // parts), (part + 1) * (Q_ORDER // parts)):
                combine(cls)
        rows = 256
        for i in range(SPAN // rows):
            rs = slice(i * rows, (i + 1) * rows)
            yb_ref[s % 2, hp, rs, :] = (ybn_ref[rs, :] * bg_ref[rs, :].astype(F32)).astype(yb_ref.dtype)


def _attn_out(qkv, bias, bg, gb, za, x2, wob, wout, *, bsz, seq):
    spans = seq // SPAN
    n_spans = bsz * spans
    t, d = x2.shape
    tq = SPAN // N_PAIRS

    in_specs = []
    for _, dil in B_PATTERNS:
        rows = SPAN // dil

        def cur_map(s, hp):
            return (hp, 0, jnp.minimum(s, n_spans - 1), 0)

        def prev_map(s, hp, rows=rows):
            return (hp, 0, jnp.maximum(jnp.minimum(s, n_spans - 1) * (rows // ATTN_BLOCK) - 1, 0), 0)

        cur = pl.BlockSpec((None, dil, rows, LANES), cur_map)
        prev = pl.BlockSpec((None, dil, ATTN_BLOCK, LANES), prev_map)
        in_specs += [cur, cur, cur, prev, prev]
    tail = pl.BlockSpec((tq, d), lambda s, hp: (jnp.maximum((s - 1) * N_PAIRS + hp, 0), 0))
    in_specs += [pl.BlockSpec(bias.shape, lambda s, hp: (0,) * bias.ndim),
                 pl.BlockSpec((None, SPAN, LANES), lambda s, hp: (hp, jnp.minimum(s, n_spans - 1), 0)),
                 tail, tail, tail, pl.BlockSpec(memory_space=pl.ANY), pl.BlockSpec(memory_space=pl.ANY)]
    args = []
    for q, k, v in qkv:
        args += [q, k, v, k, v]
    return pl.pallas_call(
        functools.partial(_attn_out_kernel, n_spans=n_spans, spans_per_seq=spans),
        grid=(n_spans + 1, N_PAIRS),
        in_specs=in_specs,
        out_specs=tail,
        out_shape=jax.ShapeDtypeStruct((t, d), F32),
        scratch_shapes=[pltpu.VMEM((len(B_PATTERNS), SPAN, LANES), F32)] * 3
        + [pltpu.VMEM((SPAN, LANES), F32), pltpu.VMEM((2, N_PAIRS, SPAN, LANES), BF16), pltpu.VMEM(wob.shape, BF16), pltpu.VMEM(wout.shape, BF16)]
        + _weight_scratch(wout.shape),
        compiler_params=_params("arbitrary", "arbitrary"),
        name="attn_out",
    )(*args, bias, bg, gb, za, x2, wob, wout)


def _band_bias():
    kj = lax.broadcasted_iota(jnp.int32, (ATTN_BLOCK, 2 * ATTN_BLOCK), 1)
    row = lax.broadcasted_iota(jnp.int32, (ATTN_BLOCK, 2 * ATTN_BLOCK), 0)
    out = []
    for _, dil in B_PATTERNS:
        groups = Q_ORDER // dil
        g_rows = ATTN_BLOCK // groups
        qi = (row % g_rows) * groups + row // g_rows
        dist = qi + ATTN_BLOCK - kj
        band = (dist >= 0) & (dist <= ATTN_BLOCK)
        first = band & (kj >= ATTN_BLOCK)
        out.append(jnp.stack([jnp.where(first, 0.0, NEG), jnp.where(band, 0.0, NEG)]))
    return jnp.stack(out).astype(F32)


def kernel(x, norm_g, w_in, a_ws, a_bs, a_ln_g, a_ln_b, b_qn_g, b_kn_g, w_oa, w_ob, w_out):
    bsz, seq, d = x.shape
    depth = w_in.shape[0]
    a_width = w_oa.shape[1]
    npat = len(B_PATTERNS)
    assert w_in.shape[2] == 3 * a_width + 3 * npat * B_WIDTH + B_WIDTH + 2 * d
    assert all(w // dil == ATTN_BLOCK and SPAN % (ATTN_BLOCK * dil) == 0 for w, dil in B_PATTERNS)
    assert seq % SPAN == 0
    t = bsz * seq
    band_bias = _band_bias()
    x2 = x.reshape(t, d)
    for l in range(depth):
        sgu_bias = jnp.repeat(a_bs[l].T, a_width // A_GROUPS, axis=1)
        gq = jnp.tile(b_qn_g[l], (1, B_HEADS)).reshape(npat, 1, B_WIDTH)
        gk = jnp.tile(b_kn_g[l], (1, B_HEADS)).reshape(npat, 1, B_WIDTH)
        res = _fused_in(x2, norm_g[l].reshape(1, d), w_in[l], a_ws[l], sgu_bias,
                        a_ln_g[l].reshape(1, -1), a_ln_b[l].reshape(1, -1), w_oa[l],
                        gq, gk)
        za, gb, bg = res[:3]
        qkv = [res[3 + 3 * p:6 + 3 * p] for p in range(npat)]
        x2 = _attn_out(qkv, band_bias, bg, gb, za, x2, w_ob[l], w_out[l],
                       bsz=bsz, seq=seq)
    return x2.reshape(bsz, seq, d)
```

```python
import functools

import jax
import jax.numpy as jnp
from jax import lax
from jax.experimental import pallas as pl
from jax.experimental.pallas import tpu as pltpu

F32 = jnp.float32
BF16 = jnp.bfloat16

EPS = 1e-6
NEG = -1e30
CHUNK = 128
A_GROUPS = 4
B_PATTERNS = ((128, 1), (512, 4), (2048, 16))
B_HEADS = 8
B_HEAD_DIM = 64
B_WIDTH = B_HEADS * B_HEAD_DIM
LANES = 128
N_PAIRS = B_WIDTH // LANES
ATTN_BLOCK = 128
SPAN = 2048
UNITS = SPAN // ATTN_BLOCK
Q_ORDER = max(d for _, d in B_PATTERNS)
CLASS_ROWS = SPAN // Q_ORDER
Q_SCALE = B_HEAD_DIM ** -0.5 * 1.4426950408889634

VMEM_LIMIT_BYTES = 60 * 1024 * 1024


def _params(*semantics):
    return pltpu.CompilerParams(dimension_semantics=semantics, vmem_limit_bytes=VMEM_LIMIT_BYTES)


def _resident(shape):
    return pl.BlockSpec(shape, lambda *_: (0,) * len(shape), pipeline_mode=pl.Buffered(1))


WEIGHT_SLOTS = 8
WEIGHT_CHUNK_BYTES = 3 << 18


def _weight_scratch(shape):
    rows, cols = shape
    chunk = 16
    while 2 * chunk * cols * 4 <= WEIGHT_CHUNK_BYTES and rows % (2 * chunk) == 0:
        chunk *= 2
    return [pltpu.VMEM((WEIGHT_SLOTS, chunk, cols), F32), pltpu.SemaphoreType.DMA((WEIGHT_SLOTS,))]


def _load_weight_bf16(src_hbm, dst_ref, stage_ref, sem):
    slots, chunk, _ = stage_ref.shape
    n = src_hbm.shape[0] // chunk

    def copy(c):
        return pltpu.make_async_copy(src_hbm.at[pl.ds(c * chunk, chunk)], stage_ref.at[c % slots],
                                     sem.at[c % slots])

    for c in range(min(slots - 1, n)):
        copy(c).start()
    for c in range(n):
        if c + slots - 1 < n:
            copy(c + slots - 1).start()
        copy(c).wait()
        dst_ref[c * chunk:(c + 1) * chunk, :] = stage_ref[c % slots].astype(BF16)


def _head_rms(t, gain):
    low = lax.broadcasted_iota(jnp.int32, (1, LANES), 1) < B_HEAD_DIM
    cols = []
    for cb in range(N_PAIRS):
        blk = t[:, cb * LANES:(cb + 1) * LANES]
        sq = blk * blk
        s0 = jnp.sum(jnp.where(low, sq, 0.0), axis=-1, keepdims=True)
        s1 = jnp.sum(jnp.where(low, 0.0, sq), axis=-1, keepdims=True)
        ms = jnp.where(low, s0, s1) * (1.0 / B_HEAD_DIM)
        cols.append(blk * lax.rsqrt(ms + EPS) * gain[:, cb * LANES:(cb + 1) * LANES])
    return jnp.concatenate(cols, axis=1)


def _emit_residue_major(val, out_ref, part, n_parts):
    dilation = out_ref.shape[1]
    rows = out_ref.shape[2] // n_parts
    for hp in range(N_PAIRS):
        for r in range(dilation):
            out_ref[hp, r, part * rows:(part + 1) * rows, :] = (
                val[r * rows:(r + 1) * rows, hp * LANES:(hp + 1) * LANES].astype(out_ref.dtype))


FUSED_IN_PARTS = 2


def _fused_in_kernel(x_ref, ng_ref, win_ref, ws_ref, sb_ref, lng_ref, lnb_ref, woa_ref, gq_ref, gk_ref,
                     za_ref, gb_ref, bg_ref, q0_ref, k0_ref, v0_ref, q1_ref, k1_ref, v1_ref,
                     q2_ref, k2_ref, v2_ref, h_ref, h4_ref, h16_ref, xs_ref, s_ref, vb_ref, yp_ref,
                     win_bf_ref, woa_bf_ref, win_stage_ref, win_sem, woa_stage_ref, woa_sem):
    tm, d_model = x_ref.shape
    a_width = woa_ref.shape[0]
    n_parts = h_ref.shape[0]
    th = tm // n_parts

    @pl.when(pl.program_id(0) == 0)
    def _():
        _load_weight_bf16(win_ref, win_bf_ref, win_stage_ref, win_sem)
        _load_weight_bf16(woa_ref, woa_bf_ref, woa_stage_ref, woa_sem)

    gw = a_width // A_GROUPS
    cbw = 256
    col_u, col_v, col_g = 0, a_width, 2 * a_width
    col_qkv = 3 * a_width
    col_bg = col_qkv + 3 * len(B_PATTERNS) * B_WIDTH
    col_ga = col_bg + B_WIDTH
    col_gb = col_ga + d_model
    h_by_dilation = {1: h_ref, 4: h4_ref, 16: h16_ref}
    outs = ((q0_ref, k0_ref, v0_ref), (q1_ref, k1_ref, v1_ref), (q2_ref, k2_ref, v2_ref))
    row = lax.broadcasted_iota(jnp.int32, (CHUNK, CHUNK), 0)
    col = lax.broadcasted_iota(jnp.int32, (CHUNK, CHUNK), 1)

    def proj(pt, col0, width, src_ref=h_ref):
        return jnp.dot(src_ref[pt], win_bf_ref[:, col0:col0 + width], preferred_element_type=F32)

    def rows_of(pt):
        return slice(pt * th, (pt + 1) * th)

    def stage_norm(pt):
        x = x_ref[rows_of(pt), :]
        ms = jnp.mean(x * x, axis=-1, keepdims=True)
        xn = x * lax.rsqrt(ms + EPS) * ng_ref[...]
        h_ref[pt] = xn.astype(BF16)
        rows4, rows16 = th // 4, th // 16
        for s in range(d_model // LANES):
            lanes = slice(s * LANES, (s + 1) * LANES)
            xs_ref[0, s] = xn[:, lanes]
            for r in range(4):
                x4 = xs_ref[0, s, pl.ds(r, rows4, stride=4), :]
                h4_ref[pt, r * rows4:(r + 1) * rows4, lanes] = x4.astype(BF16)
                xs_ref[1, s, r * rows4:(r + 1) * rows4, :] = x4
            for cls in range(16):
                x16 = xs_ref[1, s, pl.ds((cls % 4) * rows4 + cls // 4, rows16, stride=4), :]
                h16_ref[pt, cls * rows16:(cls + 1) * rows16, lanes] = x16.astype(BF16)

    def stage_v(pt):
        for cb in range(a_width // cbw):
            s_ref[pt, :, cb * cbw:(cb + 1) * cbw] = jax.nn.gelu(proj(pt, col_v + cb * cbw, cbw))

    def emit_q(q, q_ref, dil, pt):
        rows = th // Q_ORDER
        if dil == Q_ORDER:
            _emit_residue_major(q, q_ref, pt, n_parts)
        elif dil == 1:
            per_blk = ATTN_BLOCK // Q_ORDER
            for hp in range(N_PAIRS):
                for blk in range(th // ATTN_BLOCK):
                    for a in range(0, Q_ORDER, 2):
                        pair = [q[c * rows + blk * per_blk:c * rows + (blk + 1) * per_blk,
                                  hp * LANES:(hp + 1) * LANES] for c in (a, a + 1)]
                        dst = pt * th + blk * ATTN_BLOCK + a * per_blk
                        q_ref[hp, 0, dst:dst + 2 * per_blk, :] = jnp.concatenate(pair, axis=0).astype(q_ref.dtype)
        else:
            assert tm == ATTN_BLOCK * dil
            per_tile = tm // Q_ORDER
            for hp in range(N_PAIRS):
                for cls in range(Q_ORDER):
                    dst = (cls // dil) * per_tile + pt * rows
                    q_ref[hp, cls % dil, dst:dst + rows, :] = (
                        q[cls * rows:(cls + 1) * rows, hp * LANES:(hp + 1) * LANES].astype(q_ref.dtype))

    def stage_qkv(pt, p):
        q_ref, k_ref, v_ref = outs[p]
        dil = B_PATTERNS[p][1]
        hsrc = h_by_dilation[dil]
        base = col_qkv + p * 3 * B_WIDTH
        emit_q(_head_rms(proj(pt, base, B_WIDTH, h_by_dilation[Q_ORDER]), gq_ref[p] * Q_SCALE), q_ref, dil, pt)
        _emit_residue_major(_head_rms(proj(pt, base + B_WIDTH, B_WIDTH, hsrc), gk_ref[p]), k_ref, pt, n_parts)
        _emit_residue_major(proj(pt, base + 2 * B_WIDTH, B_WIDTH, hsrc), v_ref, pt, n_parts)

    def stage_layernorm(pt):
        v = s_ref[pt]
        mu = jnp.mean(v, axis=-1, keepdims=True)
        vc = v - mu
        var = jnp.mean(vc * vc, axis=-1, keepdims=True)
        vb_ref[pt] = (vc * lax.rsqrt(var + EPS) * lng_ref[...] + lnb_ref[...]).astype(BF16)

    def stage_spatial(pt):
        for g in range(A_GROUPS):
            w = jnp.where(row >= col, ws_ref[g], 0.0).astype(BF16)
            gs = slice(g * gw, (g + 1) * gw)
            for c in range(th // CHUNK):
                rs = slice(c * CHUNK, (c + 1) * CHUNK)
                s_ref[pt, rs, gs] = (jnp.dot(w, vb_ref[pt, rs, gs], preferred_element_type=F32)
                                     + sb_ref[:, gs])

    def stage_bgate(pt):
        bg = jax.nn.silu(proj(pt, col_bg, B_WIDTH))
        for hp in range(N_PAIRS):
            bg_ref[hp, rows_of(pt), :] = bg[:, hp * LANES:(hp + 1) * LANES].astype(bg_ref.dtype)

    def stage_gates(pt):
        for cb in range(a_width // cbw):
            cs = slice(cb * cbw, (cb + 1) * cbw)
            u = jax.nn.gelu(proj(pt, col_u + cb * cbw, cbw))
            gate = jax.nn.silu(proj(pt, col_g + cb * cbw, cbw))
            yp_ref[pt, :, cs] = (u * s_ref[pt, :, cs] * gate).astype(BF16)

    def stage_out(pt):
        for cb in range(d_model // cbw):
            cs = slice(cb * cbw, (cb + 1) * cbw)
            gb_ref[rows_of(pt), cs] = jax.nn.sigmoid(proj(pt, col_gb + cb * cbw, cbw)).astype(gb_ref.dtype)
            ya = jnp.dot(yp_ref[pt], woa_bf_ref[:, cs], preferred_element_type=F32)
            za_ref[rows_of(pt), cs] = (jax.nn.sigmoid(proj(pt, col_ga + cb * cbw, cbw)) * ya).astype(za_ref.dtype)

    stages = (stage_norm, stage_v, lambda pt: stage_qkv(pt, 0), stage_layernorm,
              lambda pt: stage_qkv(pt, 1), stage_spatial,
              lambda pt: (stage_qkv(pt, 2), stage_bgate(pt)), stage_gates, stage_out)
    for k in range(len(stages) + n_parts - 1):
        for pt in range(n_parts):
            if 0 <= k - pt < len(stages):
                stages[k - pt](pt)


def _fused_in(x2, ng, win, ws, sb, lng, lnb, woa, gq, gk, *, tm=512):
    t, d = x2.shape
    a_width = woa.shape[0]
    parts, th = FUSED_IN_PARTS, tm // FUSED_IN_PARTS
    row = lambda width: pl.BlockSpec((tm, width), lambda i: (i, 0))
    hbm = pl.BlockSpec(memory_space=pl.ANY)
    out_specs = [row(d), row(d), pl.BlockSpec((N_PAIRS, tm, LANES), lambda i: (0, i, 0))]
    out_shape = [jax.ShapeDtypeStruct((t, d), BF16), jax.ShapeDtypeStruct((t, d), BF16),
                 jax.ShapeDtypeStruct((N_PAIRS, t, LANES), BF16)]
    for _, dil in B_PATTERNS:
        spec = pl.BlockSpec((N_PAIRS, dil, tm // dil, LANES), lambda i: (0, 0, i, 0))
        shape = jax.ShapeDtypeStruct((N_PAIRS, dil, t // dil, LANES), BF16)
        out_specs += [spec] * 3
        out_shape += [shape] * 3
    return pl.pallas_call(
        _fused_in_kernel,
        grid=(t // tm,),
        in_specs=[row(d), _resident(ng.shape), hbm, _resident(ws.shape),
                  _resident(sb.shape), _resident(lng.shape), _resident(lnb.shape), hbm,
                  _resident(gq.shape), _resident(gk.shape)],
        out_specs=out_specs,
        out_shape=out_shape,
        scratch_shapes=[pltpu.VMEM((parts, th, d), BF16), pltpu.VMEM((parts, th, d), BF16),
                        pltpu.VMEM((parts, th, d), BF16), pltpu.VMEM((2, d // LANES, th, LANES), F32),
                        pltpu.VMEM((parts, th, a_width), F32), pltpu.VMEM((parts, th, a_width), BF16),
                        pltpu.VMEM((parts, th, a_width), BF16),
                        pltpu.VMEM(win.shape, BF16), pltpu.VMEM(woa.shape, BF16)]
        + _weight_scratch(win.shape) + _weight_scratch(woa.shape),
        compiler_params=_params("arbitrary"),
        name="fused_in",
    )(x2, ng, win, ws, sb, lng, lnb, woa, gq, gk)


def _attn_unit(q2, k2, v2, bias, low):
    n = ATTN_BLOCK
    v2e = jnp.concatenate([v2, jnp.ones_like(v2)], axis=1)
    zero = jnp.zeros_like(q2)
    qs = jnp.concatenate([jnp.where(low, q2, zero), jnp.where(low, zero, q2)], axis=0)
    s = lax.dot_general(qs, k2, (((1,), (1,)), ((), ())), preferred_element_type=F32)
    s = s + jnp.concatenate([bias, bias], axis=0)
    m = jnp.max(s, axis=-1, keepdims=True)
    e = jnp.exp2(s - m).astype(BF16)
    oe = jnp.dot(e, v2e, preferred_element_type=F32)
    mb = jnp.broadcast_to(m, (2 * n, LANES))
    return (jnp.where(low, oe[:n, :LANES], oe[n:, :LANES]),
            jnp.where(low, oe[:n, LANES:], oe[n:, LANES:]),
            jnp.where(low, mb[:n], mb[n:]))


def _attn_out_kernel(q0_ref, k0_ref, v0_ref, kp0_ref, vp0_ref, q1_ref, k1_ref, v1_ref, kp1_ref, vp1_ref,
                     q2_ref, k2_ref, v2_ref, kp2_ref, vp2_ref, bias_ref, bg_ref,
                     gb_ref, za_ref, x_ref, wob_ref, wout_ref, out_ref,
                     num_ref, den_ref, max_ref, ybn_ref, yb_ref, wob_bf_ref, wout_bf_ref, stage_ref, sem,
                     *, n_spans, spans_per_seq):
    s = pl.program_id(0)
    hp = pl.program_id(1)
    tq = out_ref.shape[0]

    @pl.when(jnp.logical_and(s == 0, hp == 0))
    def _():
        yb_ref[...] = jnp.zeros_like(yb_ref)
        _load_weight_bf16(wob_ref, wob_bf_ref, stage_ref, sem)
        _load_weight_bf16(wout_ref, wout_bf_ref, stage_ref, sem)

    def out_projection(lo, n):
        src_rows = pl.ds(pl.multiple_of(hp * tq + lo, n), n)
        yb = jnp.concatenate([yb_ref[(s + 1) % 2, j, src_rows, :] for j in range(N_PAIRS)], axis=1)
        yb = jnp.dot(yb, wob_bf_ref[...], preferred_element_type=F32)
        rs = slice(lo, lo + n)
        merged = za_ref[rs, :].astype(F32) + gb_ref[rs, :].astype(F32) * yb
        out_ref[rs, :] = x_ref[rs, :] + jnp.dot(merged.astype(BF16), wout_bf_ref[...],
                                                 preferred_element_type=F32)

    @pl.when(s == n_spans)
    def _():
        out_projection(0, tq)

    @pl.when(s < n_spans)
    def _():
        first_span = s % spans_per_seq == 0
        low = lax.broadcasted_iota(jnp.int32, (1, LANES), 1) < B_HEAD_DIM
        pats = ((q0_ref, k0_ref, v0_ref, kp0_ref, vp0_ref),
                (q1_ref, k1_ref, v1_ref, kp1_ref, vp1_ref),
                (q2_ref, k2_ref, v2_ref, kp2_ref, vp2_ref))
        def unit(p, r, c):
            q_ref, k_ref, v_ref, kp_ref, vp_ref = pats[p]
            dil = B_PATTERNS[p][1]
            rows = slice(c * ATTN_BLOCK, (c + 1) * ATTN_BLOCK)
            if c == 0:
                k2 = jnp.concatenate([kp_ref[r], k_ref[r, rows, :]], axis=0)
                v2 = jnp.concatenate([vp_ref[r], v_ref[r, rows, :]], axis=0)
                bias = bias_ref[p, jnp.where(first_span, 0, 1)]
            else:
                both = slice((c - 1) * ATTN_BLOCK, (c + 1) * ATTN_BLOCK)
                k2, v2 = k_ref[r, both, :], v_ref[r, both, :]
                bias = bias_ref[p, 1]
            tiles = _attn_unit(q_ref[r, rows, :], k2, v2, bias, low)
            groups = Q_ORDER // dil
            g_rows = ATTN_BLOCK // groups
            for g in range(groups):
                dst0 = (r + dil * g) * CLASS_ROWS + c * g_rows
                for ref, tile in zip((num_ref, den_ref, max_ref), tiles):
                    ref[p, dst0:dst0 + g_rows, :] = tile[g * g_rows:(g + 1) * g_rows]

        def combine(cls):
            rs = slice(cls * CLASS_ROWS, (cls + 1) * CLASS_ROWS)
            m0, m1, m2 = max_ref[0, rs], max_ref[1, rs], max_ref[2, rs]
            mm = jnp.maximum(jnp.maximum(m0, m1), m2)
            a0, a1, a2 = jnp.exp2(m0 - mm), jnp.exp2(m1 - mm), jnp.exp2(m2 - mm)
            num = a0 * num_ref[0, rs] + a1 * num_ref[1, rs] + a2 * num_ref[2, rs]
            den = a0 * den_ref[0, rs] + a1 * den_ref[1, rs] + a2 * den_ref[2, rs]
            ybn_ref[pl.ds(cls, CLASS_ROWS, stride=Q_ORDER), :] = num / den

        p16, p4, p1 = 2, 1, 0
        assert [B_PATTERNS[p][1] for p in (p16, p4, p1)] == [16, 4, 1]
        for r in range(16):
            unit(p16, r, 0)
        for c4 in range(UNITS // 4):
            for r in range(4):
                unit(p4, r, c4)
        for c in range(UNITS):
            unit(p1, 0, c)
        parts = 2
        for part in range(parts):
            out_projection(part * (tq // parts), tq // parts)
            for cls in range(part * (Q_ORDER // parts), (part + 1) * (Q_ORDER // parts)):
                combine(cls)
        rows = 256
        for i in range(SPAN // rows):
            rs = slice(i * rows, (i + 1) * rows)
            yb_ref[s % 2, hp, rs, :] = (ybn_ref[rs, :] * bg_ref[rs, :].astype(F32)).astype(yb_ref.dtype)


def _attn_out(qkv, bias, bg, gb, za, x2, wob, wout, *, bsz, seq):
    spans = seq // SPAN
    n_spans = bsz * spans
    t, d = x2.shape
    tq = SPAN // N_PAIRS

    in_specs = []
    for _, dil in B_PATTERNS:
        rows = SPAN // dil

        def cur_map(s, hp):
            return (hp, 0, jnp.minimum(s, n_spans - 1), 0)

        def prev_map(s, hp, rows=rows):
            return (hp, 0, jnp.maximum(jnp.minimum(s, n_spans - 1) * (rows // ATTN_BLOCK) - 1, 0), 0)

        cur = pl.BlockSpec((None, dil, rows, LANES), cur_map)
        prev = pl.BlockSpec((None, dil, ATTN_BLOCK, LANES), prev_map)
        in_specs += [cur, cur, cur, prev, prev]
    tail = pl.BlockSpec((tq, d), lambda s, hp: (jnp.maximum((s - 1) * N_PAIRS + hp, 0), 0))
    in_specs += [pl.BlockSpec(bias.shape, lambda s, hp: (0,) * bias.ndim),
                 pl.BlockSpec((None, SPAN, LANES), lambda s, hp: (hp, jnp.minimum(s, n_spans - 1), 0)),
                 tail, tail, tail, pl.BlockSpec(memory_space=pl.ANY), pl.BlockSpec(memory_space=pl.ANY)]
    args = []
    for q, k, v in qkv:
        args += [q, k, v, k, v]
    return pl.pallas_call(
        functools.partial(_attn_out_kernel, n_spans=n_spans, spans_per_seq=spans),
        grid=(n_spans + 1, N_PAIRS),
        in_specs=in_specs,
        out_specs=tail,
        out_shape=jax.ShapeDtypeStruct((t, d), F32),
        scratch_shapes=[pltpu.VMEM((len(B_PATTERNS), SPAN, LANES), F32)] * 3
        + [pltpu.VMEM((SPAN, LANES), F32), pltpu.VMEM((2, N_PAIRS, SPAN, LANES), BF16), pltpu.VMEM(wob.shape, BF16), pltpu.VMEM(wout.shape, BF16)]
        + _weight_scratch(wout.shape),
        compiler_params=_params("arbitrary", "arbitrary"),
        name="attn_out",
    )(*args, bias, bg, gb, za, x2, wob, wout)


def _band_bias():
    kj = lax.broadcasted_iota(jnp.int32, (ATTN_BLOCK, 2 * ATTN_BLOCK), 1)
    row = lax.broadcasted_iota(jnp.int32, (ATTN_BLOCK, 2 * ATTN_BLOCK), 0)
    out = []
    for _, dil in B_PATTERNS:
        groups = Q_ORDER // dil
        g_rows = ATTN_BLOCK // groups
        qi = (row % g_rows) * groups + row // g_rows
        dist = qi + ATTN_BLOCK - kj
        band = (dist >= 0) & (dist <= ATTN_BLOCK)
        first = band & (kj >= ATTN_BLOCK)
        out.append(jnp.stack([jnp.where(first, 0.0, NEG), jnp.where(band, 0.0, NEG)]))
    return jnp.stack(out).astype(F32)


def kernel(x, norm_g, w_in, a_ws, a_bs, a_ln_g, a_ln_b, b_qn_g, b_kn_g, w_oa, w_ob, w_out):
    bsz, seq, d = x.shape
    depth = w_in.shape[0]
    a_width = w_oa.shape[1]
    npat = len(B_PATTERNS)
    assert w_in.shape[2] == 3 * a_width + 3 * npat * B_WIDTH + B_WIDTH + 2 * d
    assert all(w // dil == ATTN_BLOCK and SPAN % (ATTN_BLOCK * dil) == 0 for w, dil in B_PATTERNS)
    assert seq % SPAN == 0
    t = bsz * seq
    band_bias = _band_bias()
    x2 = x.reshape(t, d)
    for l in range(depth):
        sgu_bias = jnp.repeat(a_bs[l].T, a_width // A_GROUPS, axis=1)
        gq = jnp.tile(b_qn_g[l], (1, B_HEADS)).reshape(npat, 1, B_WIDTH)
        gk = jnp.tile(b_kn_g[l], (1, B_HEADS)).reshape(npat, 1, B_WIDTH)
        res = _fused_in(x2, norm_g[l].reshape(1, d), w_in[l], a_ws[l], sgu_bias,
                        a_ln_g[l].reshape(1, -1), a_ln_b[l].reshape(1, -1), w_oa[l],
                        gq, gk)
        za, gb, bg = res[:3]
        qkv = [res[3 + 3 * p:6 + 3 * p] for p in range(npat)]
        x2 = _attn_out(qkv, band_bias, bg, gb, za, x2, w_ob[l], w_out[l],
                       bsz=bsz, seq=seq)
    return x2.reshape(bsz, seq, d)
```

```python
import functools

import jax
import jax.numpy as jnp
from jax import lax
from jax.experimental import pallas as pl
from jax.experimental.pallas import tpu as pltpu

F32 = jnp.float32
BF16 = jnp.bfloat16

EPS = 1e-6
NEG = -1e30
CHUNK = 128
A_GROUPS = 4
B_PATTERNS = ((128, 1), (512, 4), (2048, 16))
B_HEADS = 8
B_HEAD_DIM = 64
B_WIDTH = B_HEADS * B_HEAD_DIM
LANES = 128
N_PAIRS = B_WIDTH // LANES
ATTN_BLOCK = 128
SPAN = 2048
UNITS = SPAN // ATTN_BLOCK
Q_ORDER = max(d for _, d in B_PATTERNS)
CLASS_ROWS = SPAN // Q_ORDER
Q_SCALE = B_HEAD_DIM ** -0.5 * 1.4426950408889634

VMEM_LIMIT_BYTES = 60 * 1024 * 1024


def _params(*semantics):
    return pltpu.CompilerParams(dimension_semantics=semantics, vmem_limit_bytes=VMEM_LIMIT_BYTES)


def _resident(shape):
    return pl.BlockSpec(shape, lambda *_: (0,) * len(shape), pipeline_mode=pl.Buffered(1))


WEIGHT_SLOTS = 8
WEIGHT_CHUNK_BYTES = 3 << 18


def _weight_scratch(shape):
    rows, cols = shape
    chunk = 16
    while 2 * chunk * cols * 4 <= WEIGHT_CHUNK_BYTES and rows % (2 * chunk) == 0:
        chunk *= 2
    return [pltpu.VMEM((WEIGHT_SLOTS, chunk, cols), F32), pltpu.SemaphoreType.DMA((WEIGHT_SLOTS,))]


def _load_weight_bf16(src_hbm, dst_ref, stage_ref, sem):
    slots, chunk, _ = stage_ref.shape
    n = src_hbm.shape[0] // chunk

    def copy(c):
        return pltpu.make_async_copy(src_hbm.at[pl.ds(c * chunk, chunk)], stage_ref.at[c % slots],
                                     sem.at[c % slots])

    for c in range(min(slots - 1, n)):
        copy(c).start()
    for c in range(n):
        if c + slots - 1 < n:
            copy(c + slots - 1).start()
        copy(c).wait()
        dst_ref[c * chunk:(c + 1) * chunk, :] = stage_ref[c % slots].astype(BF16)


def _head_rms(t, gain):
    low = lax.broadcasted_iota(jnp.int32, (1, LANES), 1) < B_HEAD_DIM
    cols = []
    for cb in range(N_PAIRS):
        blk = t[:, cb * LANES:(cb + 1) * LANES]
        sq = blk * blk
        s0 = jnp.sum(jnp.where(low, sq, 0.0), axis=-1, keepdims=True)
        s1 = jnp.sum(jnp.where(low, 0.0, sq), axis=-1, keepdims=True)
        ms = jnp.where(low, s0, s1) * (1.0 / B_HEAD_DIM)
        cols.append(blk * lax.rsqrt(ms + EPS) * gain[:, cb * LANES:(cb + 1) * LANES])
    return jnp.concatenate(cols, axis=1)


def _emit_residue_major(val, out_ref, part, n_parts):
    dilation = out_ref.shape[1]
    rows = out_ref.shape[2] // n_parts
    for hp in range(N_PAIRS):
        for r in range(dilation):
            out_ref[hp, r, part * rows:(part + 1) * rows, :] = (
                val[r * rows:(r + 1) * rows, hp * LANES:(hp + 1) * LANES].astype(out_ref.dtype))


FUSED_IN_PARTS = 2


def _fused_in_kernel(x_ref, ng_ref, win_ref, ws_ref, sb_ref, lng_ref, lnb_ref, woa_ref, gq_ref, gk_ref,
                     za_ref, gb_ref, bg_ref, q0_ref, k0_ref, v0_ref, q1_ref, k1_ref, v1_ref,
                     q2_ref, k2_ref, v2_ref, h_ref, h4_ref, h16_ref, xs_ref, s_ref, vb_ref, yp_ref,
                     win_bf_ref, woa_bf_ref, win_stage_ref, win_sem, woa_stage_ref, woa_sem):
    tm, d_model = x_ref.shape
    a_width = woa_ref.shape[0]
    n_parts = h_ref.shape[0]
    th = tm // n_parts

    @pl.when(pl.program_id(0) == 0)
    def _():
        _load_weight_bf16(win_ref, win_bf_ref, win_stage_ref, win_sem)
        _load_weight_bf16(woa_ref, woa_bf_ref, woa_stage_ref, woa_sem)

    gw = a_width // A_GROUPS
    cbw = 512
    col_u, col_v, col_g = 0, a_width, 2 * a_width
    col_qkv = 3 * a_width
    col_bg = col_qkv + 3 * len(B_PATTERNS) * B_WIDTH
    col_ga = col_bg + B_WIDTH
    col_gb = col_ga + d_model
    h_by_dilation = {1: h_ref, 4: h4_ref, 16: h16_ref}
    outs = ((q0_ref, k0_ref, v0_ref), (q1_ref, k1_ref, v1_ref), (q2_ref, k2_ref, v2_ref))
    row = lax.broadcasted_iota(jnp.int32, (CHUNK, CHUNK), 0)
    col = lax.broadcasted_iota(jnp.int32, (CHUNK, CHUNK), 1)

    def proj(pt, col0, width, src_ref=h_ref):
        return jnp.dot(src_ref[pt], win_bf_ref[:, col0:col0 + width], preferred_element_type=F32)

    def rows_of(pt):
        return slice(pt * th, (pt + 1) * th)

    def stage_norm(pt):
        x = x_ref[rows_of(pt), :]
        ms = jnp.mean(x * x, axis=-1, keepdims=True)
        xn = x * lax.rsqrt(ms + EPS) * ng_ref[...]
        h_ref[pt] = xn.astype(BF16)
        rows4, rows16 = th // 4, th // 16
        for s in range(d_model // LANES):
            lanes = slice(s * LANES, (s + 1) * LANES)
            xs_ref[0, s] = xn[:, lanes]
            for r in range(4):
                x4 = xs_ref[0, s, pl.ds(r, rows4, stride=4), :]
                h4_ref[pt, r * rows4:(r + 1) * rows4, lanes] = x4.astype(BF16)
                xs_ref[1, s, r * rows4:(r + 1) * rows4, :] = x4
            for cls in range(16):
                x16 = xs_ref[1, s, pl.ds((cls % 4) * rows4 + cls // 4, rows16, stride=4), :]
                h16_ref[pt, cls * rows16:(cls + 1) * rows16, lanes] = x16.astype(BF16)

    def stage_v(pt):
        for cb in range(a_width // cbw):
            s_ref[pt, :, cb * cbw:(cb + 1) * cbw] = jax.nn.gelu(proj(pt, col_v + cb * cbw, cbw))

    def emit_q(q, q_ref, dil, pt):
        rows = th // Q_ORDER
        if dil == Q_ORDER:
            _emit_residue_major(q, q_ref, pt, n_parts)
        elif dil == 1:
            per_blk = ATTN_BLOCK // Q_ORDER
            for hp in range(N_PAIRS):
                for blk in range(th // ATTN_BLOCK):
                    for a in range(0, Q_ORDER, 2):
                        pair = [q[c * rows + blk * per_blk:c * rows + (blk + 1) * per_blk,
                                  hp * LANES:(hp + 1) * LANES] for c in (a, a + 1)]
                        dst = pt * th + blk * ATTN_BLOCK + a * per_blk
                        q_ref[hp, 0, dst:dst + 2 * per_blk, :] = jnp.concatenate(pair, axis=0).astype(q_ref.dtype)
        else:
            assert tm == ATTN_BLOCK * dil
            per_tile = tm // Q_ORDER
            for hp in range(N_PAIRS):
                for cls in range(Q_ORDER):
                    dst = (cls // dil) * per_tile + pt * rows
                    q_ref[hp, cls % dil, dst:dst + rows, :] = (
                        q[cls * rows:(cls + 1) * rows, hp * LANES:(hp + 1) * LANES].astype(q_ref.dtype))

    def stage_qkv(pt, p):
        q_ref, k_ref, v_ref = outs[p]
        dil = B_PATTERNS[p][1]
        hsrc = h_by_dilation[dil]
        base = col_qkv + p * 3 * B_WIDTH
        emit_q(_head_rms(proj(pt, base, B_WIDTH, h_by_dilation[Q_ORDER]), gq_ref[p] * Q_SCALE), q_ref, dil, pt)
        _emit_residue_major(_head_rms(proj(pt, base + B_WIDTH, B_WIDTH, hsrc), gk_ref[p]), k_ref, pt, n_parts)
        _emit_residue_major(proj(pt, base + 2 * B_WIDTH, B_WIDTH, hsrc), v_ref, pt, n_parts)

    def stage_layernorm(pt):
        v = s_ref[pt]
        mu = jnp.mean(v, axis=-1, keepdims=True)
        vc = v - mu
        var = jnp.mean(vc * vc, axis=-1, keepdims=True)
        vb_ref[pt] = (vc * lax.rsqrt(var + EPS) * lng_ref[...] + lnb_ref[...]).astype(BF16)

    def stage_spatial(pt):
        for g in range(A_GROUPS):
            w = jnp.where(row >= col, ws_ref[g], 0.0).astype(BF16)
            gs = slice(g * gw, (g + 1) * gw)
            for c in range(th // CHUNK):
                rs = slice(c * CHUNK, (c + 1) * CHUNK)
                s_ref[pt, rs, gs] = (jnp.dot(w, vb_ref[pt, rs, gs], preferred_element_type=F32)
                                     + sb_ref[:, gs])

    def stage_bgate(pt):
        bg = jax.nn.silu(proj(pt, col_bg, B_WIDTH))
        for hp in range(N_PAIRS):
            bg_ref[hp, rows_of(pt), :] = bg[:, hp * LANES:(hp + 1) * LANES].astype(bg_ref.dtype)

    def stage_gates(pt):
        for cb in range(a_width // cbw):
            cs = slice(cb * cbw, (cb + 1) * cbw)
            u = jax.nn.gelu(proj(pt, col_u + cb * cbw, cbw))
            gate = jax.nn.silu(proj(pt, col_g + cb * cbw, cbw))
            yp_ref[pt, :, cs] = (u * s_ref[pt, :, cs] * gate).astype(BF16)

    def stage_out(pt):
        for cb in range(d_model // cbw):
            cs = slice(cb * cbw, (cb + 1) * cbw)
            gb_ref[rows_of(pt), cs] = jax.nn.sigmoid(proj(pt, col_gb + cb * cbw, cbw)).astype(gb_ref.dtype)
            ya = jnp.dot(yp_ref[pt], woa_bf_ref[:, cs], preferred_element_type=F32)
            za_ref[rows_of(pt), cs] = (jax.nn.sigmoid(proj(pt, col_ga + cb * cbw, cbw)) * ya).astype(za_ref.dtype)

    stages = (stage_norm, stage_v, lambda pt: stage_qkv(pt, 0), stage_layernorm,
              lambda pt: stage_qkv(pt, 1), stage_spatial,
              lambda pt: (stage_qkv(pt, 2), stage_bgate(pt)), stage_gates, stage_out)
    for k in range(len(stages) + n_parts - 1):
        for pt in range(n_parts):
            if 0 <= k - pt < len(stages):
                stages[k - pt](pt)


def _fused_in(x2, ng, win, ws, sb, lng, lnb, woa, gq, gk, *, tm=512):
    t, d = x2.shape
    a_width = woa.shape[0]
    parts, th = FUSED_IN_PARTS, tm // FUSED_IN_PARTS
    row = lambda width: pl.BlockSpec((tm, width), lambda i: (i, 0))
    hbm = pl.BlockSpec(memory_space=pl.ANY)
    out_specs = [row(d), row(d), pl.BlockSpec((N_PAIRS, tm, LANES), lambda i: (0, i, 0))]
    out_shape = [jax.ShapeDtypeStruct((t, d), BF16), jax.ShapeDtypeStruct((t, d), BF16),
                 jax.ShapeDtypeStruct((N_PAIRS, t, LANES), BF16)]
    for _, dil in B_PATTERNS:
        spec = pl.BlockSpec((N_PAIRS, dil, tm // dil, LANES), lambda i: (0, 0, i, 0))
        shape = jax.ShapeDtypeStruct((N_PAIRS, dil, t // dil, LANES), BF16)
        out_specs += [spec] * 3
        out_shape += [shape] * 3
    return pl.pallas_call(
        _fused_in_kernel,
        grid=(t // tm,),
        in_specs=[row(d), _resident(ng.shape), hbm, _resident(ws.shape),
                  _resident(sb.shape), _resident(lng.shape), _resident(lnb.shape), hbm,
                  _resident(gq.shape), _resident(gk.shape)],
        out_specs=out_specs,
        out_shape=out_shape,
        scratch_shapes=[pltpu.VMEM((parts, th, d), BF16), pltpu.VMEM((parts, th, d), BF16),
                        pltpu.VMEM((parts, th, d), BF16), pltpu.VMEM((2, d // LANES, th, LANES), F32),
                        pltpu.VMEM((parts, th, a_width), F32), pltpu.VMEM((parts, th, a_width), BF16),
                        pltpu.VMEM((parts, th, a_width), BF16),
                        pltpu.VMEM(win.shape, BF16), pltpu.VMEM(woa.shape, BF16)]
        + _weight_scratch(win.shape) + _weight_scratch(woa.shape),
        compiler_params=_params("arbitrary"),
        name="fused_in",
    )(x2, ng, win, ws, sb, lng, lnb, woa, gq, gk)


def _attn_unit(q2, k2, v2, bias, low):
    n = ATTN_BLOCK
    v2e = jnp.concatenate([v2, jnp.ones_like(v2)], axis=1)
    zero = jnp.zeros_like(q2)
    qs = jnp.concatenate([jnp.where(low, q2, zero), jnp.where(low, zero, q2)], axis=0)
    s = lax.dot_general(qs, k2, (((1,), (1,)), ((), ())), preferred_element_type=F32)
    s = s + jnp.concatenate([bias, bias], axis=0)
    m = jnp.max(s, axis=-1, keepdims=True)
    e = jnp.exp2(s - m).astype(BF16)
    oe = jnp.dot(e, v2e, preferred_element_type=F32)
    mb = jnp.broadcast_to(m, (2 * n, LANES))
    return (jnp.where(low, oe[:n, :LANES], oe[n:, :LANES]),
            jnp.where(low, oe[:n, LANES:], oe[n:, LANES:]),
            jnp.where(low, mb[:n], mb[n:]))


def _attn_out_kernel(q0_ref, k0_ref, v0_ref, kp0_ref, vp0_ref, q1_ref, k1_ref, v1_ref, kp1_ref, vp1_ref,
                     q2_ref, k2_ref, v2_ref, kp2_ref, vp2_ref, bias_ref, bg_ref,
                     gb_ref, za_ref, x_ref, wob_ref, wout_ref, out_ref,
                     num_ref, den_ref, max_ref, ybn_ref, yb_ref, wob_bf_ref, wout_bf_ref, stage_ref, sem,
                     *, n_spans, spans_per_seq):
    s = pl.program_id(0)
    hp = pl.program_id(1)
    tq = out_ref.shape[0]

    @pl.when(jnp.logical_and(s == 0, hp == 0))
    def _():
        yb_ref[...] = jnp.zeros_like(yb_ref)
        _load_weight_bf16(wob_ref, wob_bf_ref, stage_ref, sem)
        _load_weight_bf16(wout_ref, wout_bf_ref, stage_ref, sem)

    def out_projection(lo, n):
        src_rows = pl.ds(pl.multiple_of(hp * tq + lo, n), n)
        yb = jnp.concatenate([yb_ref[(s + 1) % 2, j, src_rows, :] for j in range(N_PAIRS)], axis=1)
        yb = jnp.dot(yb, wob_bf_ref[...], preferred_element_type=F32)
        rs = slice(lo, lo + n)
        merged = za_ref[rs, :].astype(F32) + gb_ref[rs, :].astype(F32) * yb
        out_ref[rs, :] = x_ref[rs, :] + jnp.dot(merged.astype(BF16), wout_bf_ref[...],
                                                 preferred_element_type=F32)

    @pl.when(s == n_spans)
    def _():
        out_projection(0, tq)

    @pl.when(s < n_spans)
    def _():
        first_span = s % spans_per_seq == 0
        low = lax.broadcasted_iota(jnp.int32, (1, LANES), 1) < B_HEAD_DIM
        pats = ((q0_ref, k0_ref, v0_ref, kp0_ref, vp0_ref),
                (q1_ref, k1_ref, v1_ref, kp1_ref, vp1_ref),
                (q2_ref, k2_ref, v2_ref, kp2_ref, vp2_ref))
        def unit(p, r, c):
            q_ref, k_ref, v_ref, kp_ref, vp_ref = pats[p]
            dil = B_PATTERNS[p][1]
            rows = slice(c * ATTN_BLOCK, (c + 1) * ATTN_BLOCK)
            if c == 0:
                k2 = jnp.concatenate([kp_ref[r], k_ref[r, rows, :]], axis=0)
                v2 = jnp.concatenate([vp_ref[r], v_ref[r, rows, :]], axis=0)
                bias = bias_ref[p, jnp.where(first_span, 0, 1)]
            else:
                both = slice((c - 1) * ATTN_BLOCK, (c + 1) * ATTN_BLOCK)
                k2, v2 = k_ref[r, both, :], v_ref[r, both, :]
                bias = bias_ref[p, 1]
            tiles = _attn_unit(q_ref[r, rows, :], k2, v2, bias, low)
            groups = Q_ORDER // dil
            g_rows = ATTN_BLOCK // groups
            for g in range(groups):
                dst0 = (r + dil * g) * CLASS_ROWS + c * g_rows
                for ref, tile in zip((num_ref, den_ref, max_ref), tiles):
                    ref[p, dst0:dst0 + g_rows, :] = tile[g * g_rows:(g + 1) * g_rows]

        def combine(cls):
            rs = slice(cls * CLASS_ROWS, (cls + 1) * CLASS_ROWS)
            m0, m1, m2 = max_ref[0, rs], max_ref[1, rs], max_ref[2, rs]
            mm = jnp.maximum(jnp.maximum(m0, m1), m2)
            a0, a1, a2 = jnp.exp2(m0 - mm), jnp.exp2(m1 - mm), jnp.exp2(m2 - mm)
            num = a0 * num_ref[0, rs] + a1 * num_ref[1, rs] + a2 * num_ref[2, rs]
            den = a0 * den_ref[0, rs] + a1 * den_ref[1, rs] + a2 * den_ref[2, rs]
            ybn_ref[pl.ds(cls, CLASS_ROWS, stride=Q_ORDER), :] = num / den

        p16, p4, p1 = 2, 1, 0
        assert [B_PATTERNS[p][1] for p in (p16, p4, p1)] == [16, 4, 1]
        for r in range(16):
            unit(p16, r, 0)
        for c4 in range(UNITS // 4):
            for r in range(4):
                unit(p4, r, c4)
        for c in range(UNITS):
            unit(p1, 0, c)
        parts = 2
        for part in range(parts):
            out_projection(part * (tq // parts), tq // parts)
            for cls in range(part * (Q_ORDER // parts), (part + 1) * (Q_ORDER // parts)):
                combine(cls)
        rows = 256
        for i in range(SPAN // rows):
            rs = slice(i * rows, (i + 1) * rows)
            yb_ref[s % 2, hp, rs, :] = (ybn_ref[rs, :] * bg_ref[rs, :].astype(F32)).astype(yb_ref.dtype)


def _attn_out(qkv, bias, bg, gb, za, x2, wob, wout, *, bsz, seq):
    spans = seq // SPAN
    n_spans = bsz * spans
    t, d = x2.shape
    tq = SPAN // N_PAIRS

    in_specs = []
    for _, dil in B_PATTERNS:
        rows = SPAN // dil

        def cur_map(s, hp):
            return (hp, 0, jnp.minimum(s, n_spans - 1), 0)

        def prev_map(s, hp, rows=rows):
            return (hp, 0, jnp.maximum(jnp.minimum(s, n_spans - 1) * (rows // ATTN_BLOCK) - 1, 0), 0)

        cur = pl.BlockSpec((None, dil, rows, LANES), cur_map)
        prev = pl.BlockSpec((None, dil, ATTN_BLOCK, LANES), prev_map)
        in_specs += [cur, cur, cur, prev, prev]
    tail = pl.BlockSpec((tq, d), lambda s, hp: (jnp.maximum((s - 1) * N_PAIRS + hp, 0), 0))
    in_specs += [pl.BlockSpec(bias.shape, lambda s, hp: (0,) * bias.ndim),
                 pl.BlockSpec((None, SPAN, LANES), lambda s, hp: (hp, jnp.minimum(s, n_spans - 1), 0)),
                 tail, tail, tail, pl.BlockSpec(memory_space=pl.ANY), pl.BlockSpec(memory_space=pl.ANY)]
    args = []
    for q, k, v in qkv:
        args += [q, k, v, k, v]
    return pl.pallas_call(
        functools.partial(_attn_out_kernel, n_spans=n_spans, spans_per_seq=spans),
        grid=(n_spans + 1, N_PAIRS),
        in_specs=in_specs,
        out_specs=tail,
        out_shape=jax.ShapeDtypeStruct((t, d), F32),
        scratch_shapes=[pltpu.VMEM((len(B_PATTERNS), SPAN, LANES), F32)] * 3
        + [pltpu.VMEM((SPAN, LANES), F32), pltpu.VMEM((2, N_PAIRS, SPAN, LANES), BF16), pltpu.VMEM(wob.shape, BF16), pltpu.VMEM(wout.shape, BF16)]
        + _weight_scratch(wout.shape),
        compiler_params=_params("arbitrary", "arbitrary"),
        name="attn_out",
    )(*args, bias, bg, gb, za, x2, wob, wout)


def _band_bias():
    kj = lax.broadcasted_iota(jnp.int32, (ATTN_BLOCK, 2 * ATTN_BLOCK), 1)
    row = lax.broadcasted_iota(jnp.int32, (ATTN_BLOCK, 2 * ATTN_BLOCK), 0)
    out = []
    for _, dil in B_PATTERNS:
        groups = Q_ORDER // dil
        g_rows = ATTN_BLOCK // groups
        qi = (row % g_rows) * groups + row // g_rows
        dist = qi + ATTN_BLOCK - kj
        band = (dist >= 0) & (dist <= ATTN_BLOCK)
        first = band & (kj >= ATTN_BLOCK)
        out.append(jnp.stack([jnp.where(first, 0.0, NEG), jnp.where(band, 0.0, NEG)]))
    return jnp.stack(out).astype(F32)


def kernel(x, norm_g, w_in, a_ws, a_bs, a_ln_g, a_ln_b, b_qn_g, b_kn_g, w_oa, w_ob, w_out):
    bsz, seq, d = x.shape
    depth = w_in.shape[0]
    a_width = w_oa.shape[1]
    npat = len(B_PATTERNS)
    assert w_in.shape[2] == 3 * a_width + 3 * npat * B_WIDTH + B_WIDTH + 2 * d
    assert all(w // dil == ATTN_BLOCK and SPAN % (ATTN_BLOCK * dil) == 0 for w, dil in B_PATTERNS)
    assert seq % SPAN == 0
    t = bsz * seq
    band_bias = _band_bias()
    x2 = x.reshape(t, d)
    for l in range(depth):
        sgu_bias = jnp.repeat(a_bs[l].T, a_width // A_GROUPS, axis=1)
        gq = jnp.tile(b_qn_g[l], (1, B_HEADS)).reshape(npat, 1, B_WIDTH)
        gk = jnp.tile(b_kn_g[l], (1, B_HEADS)).reshape(npat, 1, B_WIDTH)
        res = _fused_in(x2, norm_g[l].reshape(1, d), w_in[l], a_ws[l], sgu_bias,
                        a_ln_g[l].reshape(1, -1), a_ln_b[l].reshape(1, -1), w_oa[l],
                        gq, gk)
        za, gb, bg = res[:3]
        qkv = [res[3 + 3 * p:6 + 3 * p] for p in range(npat)]
        x2 = _attn_out(qkv, band_bias, bg, gb, za, x2, w_ob[l], w_out[l],
                       bsz=bsz, seq=seq)
    return x2.reshape(bsz, seq, d)
```

```python
import functools

import jax
import jax.numpy as jnp
from jax import lax
from jax.experimental import pallas as pl
from jax.experimental.pallas import tpu as pltpu

F32 = jnp.float32
BF16 = jnp.bfloat16

EPS = 1e-6
NEG = -1e30
CHUNK = 128
A_GROUPS = 4
B_PATTERNS = ((128, 1), (512, 4), (2048, 16))
B_HEADS = 8
B_HEAD_DIM = 64
B_WIDTH = B_HEADS * B_HEAD_DIM
LANES = 128
N_PAIRS = B_WIDTH // LANES
ATTN_BLOCK = 128
SPAN = 2048
UNITS = SPAN // ATTN_BLOCK
Q_ORDER = max(d for _, d in B_PATTERNS)
CLASS_ROWS = SPAN // Q_ORDER
Q_SCALE = B_HEAD_DIM ** -0.5 * 1.4426950408889634

VMEM_LIMIT_BYTES = 60 * 1024 * 1024


def _params(*semantics):
    return pltpu.CompilerParams(dimension_semantics=semantics, vmem_limit_bytes=VMEM_LIMIT_BYTES)


def _resident(shape):
    return pl.BlockSpec(shape, lambda *_: (0,) * len(shape), pipeline_mode=pl.Buffered(1))


WEIGHT_SLOTS = 8
WEIGHT_CHUNK_BYTES = 3 << 18


def _weight_scratch(shape):
    rows, cols = shape
    chunk = 16
    while 2 * chunk * cols * 4 <= WEIGHT_CHUNK_BYTES and rows % (2 * chunk) == 0:
        chunk *= 2
    return [pltpu.VMEM((WEIGHT_SLOTS, chunk, cols), F32), pltpu.SemaphoreType.DMA((WEIGHT_SLOTS,))]


def _load_weight_bf16(src_hbm, dst_ref, stage_ref, sem):
    slots, chunk, _ = stage_ref.shape
    n = src_hbm.shape[0] // chunk

    def copy(c):
        return pltpu.make_async_copy(src_hbm.at[pl.ds(c * chunk, chunk)], stage_ref.at[c % slots],
                                     sem.at[c % slots])

    for c in range(min(slots - 1, n)):
        copy(c).start(priority=c % 2)
    for c in range(n):
        if c + slots - 1 < n:
            copy(c + slots - 1).start(priority=(c + slots - 1) % 2)
        copy(c).wait()
        dst_ref[c * chunk:(c + 1) * chunk, :] = stage_ref[c % slots].astype(BF16)


def _head_rms(t, gain):
    low = lax.broadcasted_iota(jnp.int32, (1, LANES), 1) < B_HEAD_DIM
    cols = []
    for cb in range(N_PAIRS):
        blk = t[:, cb * LANES:(cb + 1) * LANES]
        sq = blk * blk
        s0 = jnp.sum(jnp.where(low, sq, 0.0), axis=-1, keepdims=True)
        s1 = jnp.sum(jnp.where(low, 0.0, sq), axis=-1, keepdims=True)
        ms = jnp.where(low, s0, s1) * (1.0 / B_HEAD_DIM)
        cols.append(blk * lax.rsqrt(ms + EPS) * gain[:, cb * LANES:(cb + 1) * LANES])
    return jnp.concatenate(cols, axis=1)


def _emit_residue_major(val, out_ref, part, n_parts):
    dilation = out_ref.shape[1]
    rows = out_ref.shape[2] // n_parts
    for hp in range(N_PAIRS):
        for r in range(dilation):
            out_ref[hp, r, part * rows:(part + 1) * rows, :] = (
                val[r * rows:(r + 1) * rows, hp * LANES:(hp + 1) * LANES].astype(out_ref.dtype))


FUSED_IN_PARTS = 2


def _fused_in_kernel(x_ref, ng_ref, win_ref, ws_ref, sb_ref, lng_ref, lnb_ref, woa_ref, gq_ref, gk_ref,
                     za_ref, gb_ref, bg_ref, q0_ref, k0_ref, v0_ref, q1_ref, k1_ref, v1_ref,
                     q2_ref, k2_ref, v2_ref, h_ref, h4_ref, h16_ref, xs_ref, s_ref, vb_ref, yp_ref,
                     win_bf_ref, woa_bf_ref, win_stage_ref, win_sem, woa_stage_ref, woa_sem):
    tm, d_model = x_ref.shape
    a_width = woa_ref.shape[0]
    n_parts = h_ref.shape[0]
    th = tm // n_parts

    @pl.when(pl.program_id(0) == 0)
    def _():
        _load_weight_bf16(win_ref, win_bf_ref, win_stage_ref, win_sem)
        _load_weight_bf16(woa_ref, woa_bf_ref, woa_stage_ref, woa_sem)

    gw = a_width // A_GROUPS
    cbw = 512
    col_u, col_v, col_g = 0, a_width, 2 * a_width
    col_qkv = 3 * a_width
    col_bg = col_qkv + 3 * len(B_PATTERNS) * B_WIDTH
    col_ga = col_bg + B_WIDTH
    col_gb = col_ga + d_model
    h_by_dilation = {1: h_ref, 4: h4_ref, 16: h16_ref}
    outs = ((q0_ref, k0_ref, v0_ref), (q1_ref, k1_ref, v1_ref), (q2_ref, k2_ref, v2_ref))
    row = lax.broadcasted_iota(jnp.int32, (CHUNK, CHUNK), 0)
    col = lax.broadcasted_iota(jnp.int32, (CHUNK, CHUNK), 1)

    def proj(pt, col0, width, src_ref=h_ref):
        return jnp.dot(src_ref[pt], win_bf_ref[:, col0:col0 + width], preferred_element_type=F32)

    def rows_of(pt):
        return slice(pt * th, (pt + 1) * th)

    def stage_norm(pt):
        x = x_ref[rows_of(pt), :]
        ms = jnp.mean(x * x, axis=-1, keepdims=True)
        xn = x * lax.rsqrt(ms + EPS) * ng_ref[...]
        h_ref[pt] = xn.astype(BF16)
        rows4, rows16 = th // 4, th // 16
        for s in range(d_model // LANES):
            lanes = slice(s * LANES, (s + 1) * LANES)
            xs_ref[0, s] = xn[:, lanes]
            for r in range(4):
                x4 = xs_ref[0, s, pl.ds(r, rows4, stride=4), :]
                h4_ref[pt, r * rows4:(r + 1) * rows4, lanes] = x4.astype(BF16)
                xs_ref[1, s, r * rows4:(r + 1) * rows4, :] = x4
            for cls in range(16):
                x16 = xs_ref[1, s, pl.ds((cls % 4) * rows4 + cls // 4, rows16, stride=4), :]
                h16_ref[pt, cls * rows16:(cls + 1) * rows16, lanes] = x16.astype(BF16)

    def stage_v(pt):
        for cb in range(a_width // cbw):
            s_ref[pt, :, cb * cbw:(cb + 1) * cbw] = jax.nn.gelu(proj(pt, col_v + cb * cbw, cbw))

    def emit_q(q, q_ref, dil, pt):
        rows = th // Q_ORDER
        if dil == Q_ORDER:
            _emit_residue_major(q, q_ref, pt, n_parts)
        elif dil == 1:
            per_blk = ATTN_BLOCK // Q_ORDER
            for hp in range(N_PAIRS):
                for blk in range(th // ATTN_BLOCK):
                    for a in range(0, Q_ORDER, 2):
                        pair = [q[c * rows + blk * per_blk:c * rows + (blk + 1) * per_blk,
                                  hp * LANES:(hp + 1) * LANES] for c in (a, a + 1)]
                        dst = pt * th + blk * ATTN_BLOCK + a * per_blk
                        q_ref[hp, 0, dst:dst + 2 * per_blk, :] = jnp.concatenate(pair, axis=0).astype(q_ref.dtype)
        else:
            assert tm == ATTN_BLOCK * dil
            per_tile = tm // Q_ORDER
            for hp in range(N_PAIRS):
                for cls in range(Q_ORDER):
                    dst = (cls // dil) * per_tile + pt * rows
                    q_ref[hp, cls % dil, dst:dst + rows, :] = (
                        q[cls * rows:(cls + 1) * rows, hp * LANES:(hp + 1) * LANES].astype(q_ref.dtype))

    def stage_qkv(pt, p):
        q_ref, k_ref, v_ref = outs[p]
        dil = B_PATTERNS[p][1]
        hsrc = h_by_dilation[dil]
        base = col_qkv + p * 3 * B_WIDTH
        emit_q(_head_rms(proj(pt, base, B_WIDTH, h_by_dilation[Q_ORDER]), gq_ref[p] * Q_SCALE), q_ref, dil, pt)
        _emit_residue_major(_head_rms(proj(pt, base + B_WIDTH, B_WIDTH, hsrc), gk_ref[p]), k_ref, pt, n_parts)
        _emit_residue_major(proj(pt, base + 2 * B_WIDTH, B_WIDTH, hsrc), v_ref, pt, n_parts)

    def stage_layernorm(pt):
        v = s_ref[pt]
        mu = jnp.mean(v, axis=-1, keepdims=True)
        vc = v - mu
        var = jnp.mean(vc * vc, axis=-1, keepdims=True)
        vb_ref[pt] = (vc * lax.rsqrt(var + EPS) * lng_ref[...] + lnb_ref[...]).astype(BF16)

    def stage_spatial(pt):
        for g in range(A_GROUPS):
            w = jnp.where(row >= col, ws_ref[g], 0.0).astype(BF16)
            gs = slice(g * gw, (g + 1) * gw)
            for c in range(th // CHUNK):
                rs = slice(c * CHUNK, (c + 1) * CHUNK)
                s_ref[pt, rs, gs] = (jnp.dot(w, vb_ref[pt, rs, gs], preferred_element_type=F32)
                                     + sb_ref[:, gs])

    def stage_bgate(pt):
        bg = jax.nn.silu(proj(pt, col_bg, B_WIDTH))
        for hp in range(N_PAIRS):
            bg_ref[hp, rows_of(pt), :] = bg[:, hp * LANES:(hp + 1) * LANES].astype(bg_ref.dtype)

    def stage_gates(pt):
        for cb in range(a_width // cbw):
            cs = slice(cb * cbw, (cb + 1) * cbw)
            u = jax.nn.gelu(proj(pt, col_u + cb * cbw, cbw))
            gate = jax.nn.silu(proj(pt, col_g + cb * cbw, cbw))
            yp_ref[pt, :, cs] = (u * s_ref[pt, :, cs] * gate).astype(BF16)

    def stage_out(pt):
        for cb in range(d_model // cbw):
            cs = slice(cb * cbw, (cb + 1) * cbw)
            gb_ref[rows_of(pt), cs] = jax.nn.sigmoid(proj(pt, col_gb + cb * cbw, cbw)).astype(gb_ref.dtype)
            ya = jnp.dot(yp_ref[pt], woa_bf_ref[:, cs], preferred_element_type=F32)
            za_ref[rows_of(pt), cs] = (jax.nn.sigmoid(proj(pt, col_ga + cb * cbw, cbw)) * ya).astype(za_ref.dtype)

    stages = (stage_norm, stage_v, lambda pt: stage_qkv(pt, 0), stage_layernorm,
              lambda pt: stage_qkv(pt, 1), stage_spatial,
              lambda pt: (stage_qkv(pt, 2), stage_bgate(pt)), stage_gates, stage_out)
    for k in range(len(stages) + n_parts - 1):
        for pt in range(n_parts):
            if 0 <= k - pt < len(stages):
                stages[k - pt](pt)


def _fused_in(x2, ng, win, ws, sb, lng, lnb, woa, gq, gk, *, tm=512):
    t, d = x2.shape
    a_width = woa.shape[0]
    parts, th = FUSED_IN_PARTS, tm // FUSED_IN_PARTS
    row = lambda width: pl.BlockSpec((tm, width), lambda i: (i, 0))
    hbm = pl.BlockSpec(memory_space=pl.ANY)
    out_specs = [row(d), row(d), pl.BlockSpec((N_PAIRS, tm, LANES), lambda i: (0, i, 0))]
    out_shape = [jax.ShapeDtypeStruct((t, d), BF16), jax.ShapeDtypeStruct((t, d), BF16),
                 jax.ShapeDtypeStruct((N_PAIRS, t, LANES), BF16)]
    for _, dil in B_PATTERNS:
        spec = pl.BlockSpec((N_PAIRS, dil, tm // dil, LANES), lambda i: (0, 0, i, 0))
        shape = jax.ShapeDtypeStruct((N_PAIRS, dil, t // dil, LANES), BF16)
        out_specs += [spec] * 3
        out_shape += [shape] * 3
    return pl.pallas_call(
        _fused_in_kernel,
        grid=(t // tm,),
        in_specs=[row(d), _resident(ng.shape), hbm, _resident(ws.shape),
                  _resident(sb.shape), _resident(lng.shape), _resident(lnb.shape), hbm,
                  _resident(gq.shape), _resident(gk.shape)],
        out_specs=out_specs,
        out_shape=out_shape,
        scratch_shapes=[pltpu.VMEM((parts, th, d), BF16), pltpu.VMEM((parts, th, d), BF16),
                        pltpu.VMEM((parts, th, d), BF16), pltpu.VMEM((2, d // LANES, th, LANES), F32),
                        pltpu.VMEM((parts, th, a_width), F32), pltpu.VMEM((parts, th, a_width), BF16),
                        pltpu.VMEM((parts, th, a_width), BF16),
                        pltpu.VMEM(win.shape, BF16), pltpu.VMEM(woa.shape, BF16)]
        + _weight_scratch(win.shape) + _weight_scratch(woa.shape),
        compiler_params=_params("arbitrary"),
        name="fused_in",
    )(x2, ng, win, ws, sb, lng, lnb, woa, gq, gk)


def _attn_unit(q2, k2, v2, bias, low):
    n = ATTN_BLOCK
    v2e = jnp.concatenate([v2, jnp.ones_like(v2)], axis=1)
    zero = jnp.zeros_like(q2)
    qs = jnp.concatenate([jnp.where(low, q2, zero), jnp.where(low, zero, q2)], axis=0)
    s = lax.dot_general(qs, k2, (((1,), (1,)), ((), ())), preferred_element_type=F32)
    s = s + jnp.concatenate([bias, bias], axis=0)
    m = jnp.max(s, axis=-1, keepdims=True)
    e = jnp.exp2(s - m).astype(BF16)
    oe = jnp.dot(e, v2e, preferred_element_type=F32)
    mb = jnp.broadcast_to(m, (2 * n, LANES))
    return (jnp.where(low, oe[:n, :LANES], oe[n:, :LANES]),
            jnp.where(low, oe[:n, LANES:], oe[n:, LANES:]),
            jnp.where(low, mb[:n], mb[n:]))


def _attn_out_kernel(q0_ref, k0_ref, v0_ref, kp0_ref, vp0_ref, q1_ref, k1_ref, v1_ref, kp1_ref, vp1_ref,
                     q2_ref, k2_ref, v2_ref, kp2_ref, vp2_ref, bias_ref, bg_ref,
                     gb_ref, za_ref, x_ref, wob_ref, wout_ref, out_ref,
                     num_ref, den_ref, max_ref, ybn_ref, yb_ref, wob_bf_ref, wout_bf_ref, stage_ref, sem,
                     *, n_spans, spans_per_seq):
    s = pl.program_id(0)
    hp = pl.program_id(1)
    tq = out_ref.shape[0]

    @pl.when(jnp.logical_and(s == 0, hp == 0))
    def _():
        yb_ref[...] = jnp.zeros_like(yb_ref)
        _load_weight_bf16(wob_ref, wob_bf_ref, stage_ref, sem)
        _load_weight_bf16(wout_ref, wout_bf_ref, stage_ref, sem)

    def out_projection(lo, n):
        src_rows = pl.ds(pl.multiple_of(hp * tq + lo, n), n)
        yb = jnp.concatenate([yb_ref[(s + 1) % 2, j, src_rows, :] for j in range(N_PAIRS)], axis=1)
        yb = jnp.dot(yb, wob_bf_ref[...], preferred_element_type=F32)
        rs = slice(lo, lo + n)
        merged = za_ref[rs, :].astype(F32) + gb_ref[rs, :].astype(F32) * yb
        out_ref[rs, :] = x_ref[rs, :] + jnp.dot(merged.astype(BF16), wout_bf_ref[...],
                                                 preferred_element_type=F32)

    @pl.when(s == n_spans)
    def _():
        out_projection(0, tq)

    @pl.when(s < n_spans)
    def _():
        first_span = s % spans_per_seq == 0
        low = lax.broadcasted_iota(jnp.int32, (1, LANES), 1) < B_HEAD_DIM
        pats = ((q0_ref, k0_ref, v0_ref, kp0_ref, vp0_ref),
                (q1_ref, k1_ref, v1_ref, kp1_ref, vp1_ref),
                (q2_ref, k2_ref, v2_ref, kp2_ref, vp2_ref))
        def unit(p, r, c):
            q_ref, k_ref, v_ref, kp_ref, vp_ref = pats[p]
            dil = B_PATTERNS[p][1]
            rows = slice(c * ATTN_BLOCK, (c + 1) * ATTN_BLOCK)
            if c == 0:
                k2 = jnp.concatenate([kp_ref[r], k_ref[r, rows, :]], axis=0)
                v2 = jnp.concatenate([vp_ref[r], v_ref[r, rows, :]], axis=0)
                bias = bias_ref[p, jnp.where(first_span, 0, 1)]
            else:
                both = slice((c - 1) * ATTN_BLOCK, (c + 1) * ATTN_BLOCK)
                k2, v2 = k_ref[r, both, :], v_ref[r, both, :]
                bias = bias_ref[p, 1]
            tiles = _attn_unit(q_ref[r, rows, :], k2, v2, bias, low)
            groups = Q_ORDER // dil
            g_rows = ATTN_BLOCK // groups
            for g in range(groups):
                dst0 = (r + dil * g) * CLASS_ROWS + c * g_rows
                for ref, tile in zip((num_ref, den_ref, max_ref), tiles):
                    ref[p, dst0:dst0 + g_rows, :] = tile[g * g_rows:(g + 1) * g_rows]

        def combine(cls):
            rs = slice(cls * CLASS_ROWS, (cls + 1) * CLASS_ROWS)
            m0, m1, m2 = max_ref[0, rs], max_ref[1, rs], max_ref[2, rs]
            mm = jnp.maximum(jnp.maximum(m0, m1), m2)
            a0, a1, a2 = jnp.exp2(m0 - mm), jnp.exp2(m1 - mm), jnp.exp2(m2 - mm)
            num = a0 * num_ref[0, rs] + a1 * num_ref[1, rs] + a2 * num_ref[2, rs]
            den = a0 * den_ref[0, rs] + a1 * den_ref[1, rs] + a2 * den_ref[2, rs]
            ybn_ref[pl.ds(cls, CLASS_ROWS, stride=Q_ORDER), :] = num / den

        p16, p4, p1 = 2, 1, 0
        assert [B_PATTERNS[p][1] for p in (p16, p4, p1)] == [16, 4, 1]
        for r in range(16):
            unit(p16, r, 0)
        for c4 in range(UNITS // 4):
            for r in range(4):
                unit(p4, r, c4)
        for c in range(UNITS):
            unit(p1, 0, c)
        parts = 2
        for part in range(parts):
            out_projection(part * (tq // parts), tq // parts)
            for cls in range(part * (Q_ORDER // parts), (part + 1) * (Q_ORDER // parts)):
                combine(cls)
        rows = 256
        for i in range(SPAN // rows):
            rs = slice(i * rows, (i + 1) * rows)
            yb_ref[s % 2, hp, rs, :] = (ybn_ref[rs, :] * bg_ref[rs, :].astype(F32)).astype(yb_ref.dtype)


def _attn_out(qkv, bias, bg, gb, za, x2, wob, wout, *, bsz, seq):
    spans = seq // SPAN
    n_spans = bsz * spans
    t, d = x2.shape
    tq = SPAN // N_PAIRS

    in_specs = []
    for _, dil in B_PATTERNS:
        rows = SPAN // dil

        def cur_map(s, hp):
            return (hp, 0, jnp.minimum(s, n_spans - 1), 0)

        def prev_map(s, hp, rows=rows):
            return (hp, 0, jnp.maximum(jnp.minimum(s, n_spans - 1) * (rows // ATTN_BLOCK) - 1, 0), 0)

        cur = pl.BlockSpec((None, dil, rows, LANES), cur_map)
        prev = pl.BlockSpec((None, dil, ATTN_BLOCK, LANES), prev_map)
        in_specs += [cur, cur, cur, prev, prev]
    tail = pl.BlockSpec((tq, d), lambda s, hp: (jnp.maximum((s - 1) * N_PAIRS + hp, 0), 0))
    in_specs += [pl.BlockSpec(bias.shape, lambda s, hp: (0,) * bias.ndim),
                 pl.BlockSpec((None, SPAN, LANES), lambda s, hp: (hp, jnp.minimum(s, n_spans - 1), 0)),
                 tail, tail, tail, pl.BlockSpec(memory_space=pl.ANY), pl.BlockSpec(memory_space=pl.ANY)]
    args = []
    for q, k, v in qkv:
        args += [q, k, v, k, v]
    return pl.pallas_call(
        functools.partial(_attn_out_kernel, n_spans=n_spans, spans_per_seq=spans),
        grid=(n_spans + 1, N_PAIRS),
        in_specs=in_specs,
        out_specs=tail,
        out_shape=jax.ShapeDtypeStruct((t, d), F32),
        scratch_shapes=[pltpu.VMEM((len(B_PATTERNS), SPAN, LANES), F32)] * 3
        + [pltpu.VMEM((SPAN, LANES), F32), pltpu.VMEM((2, N_PAIRS, SPAN, LANES), BF16), pltpu.VMEM(wob.shape, BF16), pltpu.VMEM(wout.shape, BF16)]
        + _weight_scratch(wout.shape),
        compiler_params=_params("arbitrary", "arbitrary"),
        name="attn_out",
    )(*args, bias, bg, gb, za, x2, wob, wout)


def _band_bias():
    kj = lax.broadcasted_iota(jnp.int32, (ATTN_BLOCK, 2 * ATTN_BLOCK), 1)
    row = lax.broadcasted_iota(jnp.int32, (ATTN_BLOCK, 2 * ATTN_BLOCK), 0)
    out = []
    for _, dil in B_PATTERNS:
        groups = Q_ORDER // dil
        g_rows = ATTN_BLOCK // groups
        qi = (row % g_rows) * groups + row // g_rows
        dist = qi + ATTN_BLOCK - kj
        band = (dist >= 0) & (dist <= ATTN_BLOCK)
        first = band & (kj >= ATTN_BLOCK)
        out.append(jnp.stack([jnp.where(first, 0.0, NEG), jnp.where(band, 0.0, NEG)]))
    return jnp.stack(out).astype(F32)


def kernel(x, norm_g, w_in, a_ws, a_bs, a_ln_g, a_ln_b, b_qn_g, b_kn_g, w_oa, w_ob, w_out):
    bsz, seq, d = x.shape
    depth = w_in.shape[0]
    a_width = w_oa.shape[1]
    npat = len(B_PATTERNS)
    assert w_in.shape[2] == 3 * a_width + 3 * npat * B_WIDTH + B_WIDTH + 2 * d
    assert all(w // dil == ATTN_BLOCK and SPAN % (ATTN_BLOCK * dil) == 0 for w, dil in B_PATTERNS)
    assert seq % SPAN == 0
    t = bsz * seq
    band_bias = _band_bias()
    x2 = x.reshape(t, d)
    for l in range(depth):
        sgu_bias = jnp.repeat(a_bs[l].T, a_width // A_GROUPS, axis=1)
        gq = jnp.tile(b_qn_g[l], (1, B_HEADS)).reshape(npat, 1, B_WIDTH)
        gk = jnp.tile(b_kn_g[l], (1, B_HEADS)).reshape(npat, 1, B_WIDTH)
        res = _fused_in(x2, norm_g[l].reshape(1, d), w_in[l], a_ws[l], sgu_bias,
                        a_ln_g[l].reshape(1, -1), a_ln_b[l].reshape(1, -1), w_oa[l],
                        gq, gk)
        za, gb, bg = res[:3]
        qkv = [res[3 + 3 * p:6 + 3 * p] for p in range(npat)]
        x2 = _attn_out(qkv, band_bias, bg, gb, za, x2, w_ob[l], w_out[l],
                       bsz=bsz, seq=seq)
    return x2.reshape(bsz, seq, d)
```

```python
import functools

import jax
import jax.numpy as jnp
from jax import lax
from jax.experimental import pallas as pl
from jax.experimental.pallas import tpu as pltpu

F32 = jnp.float32
BF16 = jnp.bfloat16

EPS = 1e-6
NEG = -1e30
CHUNK = 128
A_GROUPS = 4
B_PATTERNS = ((128, 1), (512, 4), (2048, 16))
B_HEADS = 8
B_HEAD_DIM = 64
B_WIDTH = B_HEADS * B_HEAD_DIM
LANES = 128
N_PAIRS = B_WIDTH // LANES
ATTN_BLOCK = 128
SPAN = 2048
UNITS = SPAN // ATTN_BLOCK
Q_ORDER = max(d for _, d in B_PATTERNS)
CLASS_ROWS = SPAN // Q_ORDER
Q_SCALE = B_HEAD_DIM ** -0.5 * 1.4426950408889634

VMEM_LIMIT_BYTES = 60 * 1024 * 1024


def _params(*semantics):
    return pltpu.CompilerParams(dimension_semantics=semantics, vmem_limit_bytes=VMEM_LIMIT_BYTES)


def _resident(shape):
    return pl.BlockSpec(shape, lambda *_: (0,) * len(shape), pipeline_mode=pl.Buffered(1))


WEIGHT_SLOTS = 8
WEIGHT_CHUNK_BYTES = 3 << 18


def _weight_scratch(shape):
    rows, cols = shape
    chunk = 16
    while 2 * chunk * cols * 4 <= WEIGHT_CHUNK_BYTES and rows % (2 * chunk) == 0:
        chunk *= 2
    return [pltpu.VMEM((WEIGHT_SLOTS, chunk, cols), F32), pltpu.SemaphoreType.DMA((WEIGHT_SLOTS,))]


def _load_weight_bf16(src_hbm, dst_ref, stage_ref, sem):
    slots, chunk, _ = stage_ref.shape
    n = src_hbm.shape[0] // chunk

    def copy(c):
        return pltpu.make_async_copy(src_hbm.at[pl.ds(c * chunk, chunk)], stage_ref.at[c % slots],
                                     sem.at[c % slots])

    for c in range(min(slots - 1, n)):
        copy(c).start()
    for c in range(n):
        if c + slots - 1 < n:
            copy(c + slots - 1).start()
        copy(c).wait()
        dst_ref[c * chunk:(c + 1) * chunk, :] = stage_ref[c % slots].astype(BF16)


def _head_rms(t, gain):
    low = lax.broadcasted_iota(jnp.int32, (1, LANES), 1) < B_HEAD_DIM
    cols = []
    for cb in range(N_PAIRS):
        blk = t[:, cb * LANES:(cb + 1) * LANES]
        sq = blk * blk
        s0 = jnp.sum(jnp.where(low, sq, 0.0), axis=-1, keepdims=True)
        s1 = jnp.sum(jnp.where(low, 0.0, sq), axis=-1, keepdims=True)
        ms = jnp.where(low, s0, s1) * (1.0 / B_HEAD_DIM)
        cols.append(blk * lax.rsqrt(ms + EPS) * gain[:, cb * LANES:(cb + 1) * LANES])
    return jnp.concatenate(cols, axis=1)


def _emit_residue_major(val, out_ref, part, n_parts):
    dilation = out_ref.shape[1]
    rows = out_ref.shape[2] // n_parts
    for hp in range(N_PAIRS):
        for r in range(dilation):
            out_ref[hp, r, part * rows:(part + 1) * rows, :] = (
                val[r * rows:(r + 1) * rows, hp * LANES:(hp + 1) * LANES].astype(out_ref.dtype))


FUSED_IN_PARTS = 2


def _fused_in_kernel(x_ref, ng_ref, win_ref, ws_ref, sb_ref, lng_ref, lnb_ref, woa_ref, gq_ref, gk_ref,
                     za_ref, gb_ref, bg_ref, q0_ref, k0_ref, v0_ref, q1_ref, k1_ref, v1_ref,
                     q2_ref, k2_ref, v2_ref, h_ref, h4_ref, h16_ref, xs_ref, s_ref, vb_ref, yp_ref,
                     win_bf_ref, woa_bf_ref, win_stage_ref, win_sem, woa_stage_ref, woa_sem):
    tm, d_model = x_ref.shape
    a_width = woa_ref.shape[0]
    n_parts = h_ref.shape[0]
    th = tm // n_parts

    @pl.when(pl.program_id(0) == 0)
    def _():
        _load_weight_bf16(win_ref, win_bf_ref, win_stage_ref, win_sem)
        _load_weight_bf16(woa_ref, woa_bf_ref, woa_stage_ref, woa_sem)

    gw = a_width // A_GROUPS
    cbw = 512
    col_u, col_v, col_g = 0, a_width, 2 * a_width
    col_qkv = 3 * a_width
    col_bg = col_qkv + 3 * len(B_PATTERNS) * B_WIDTH
    col_ga = col_bg + B_WIDTH
    col_gb = col_ga + d_model
    h_by_dilation = {1: h_ref, 4: h4_ref, 16: h16_ref}
    outs = ((q0_ref, k0_ref, v0_ref), (q1_ref, k1_ref, v1_ref), (q2_ref, k2_ref, v2_ref))
    row = lax.broadcasted_iota(jnp.int32, (CHUNK, CHUNK), 0)
    col = lax.broadcasted_iota(jnp.int32, (CHUNK, CHUNK), 1)

    def proj(pt, col0, width, src_ref=h_ref):
        return jnp.dot(src_ref[pt], win_bf_ref[:, col0:col0 + width], preferred_element_type=F32)

    def rows_of(pt):
        return slice(pt * th, (pt + 1) * th)

    def stage_norm(pt):
        x = x_ref[rows_of(pt), :]
        ms = jnp.mean(x * x, axis=-1, keepdims=True)
        xn = x * lax.rsqrt(ms + EPS) * ng_ref[...]
        h_ref[pt] = xn.astype(BF16)
        rows4, rows16 = th // 4, th // 16
        for s in range(d_model // LANES):
            lanes = slice(s * LANES, (s + 1) * LANES)
            xs_ref[0, s] = xn[:, lanes]
            for r in range(4):
                x4 = xs_ref[0, s, pl.ds(r, rows4, stride=4), :]
                h4_ref[pt, r * rows4:(r + 1) * rows4, lanes] = x4.astype(BF16)
                xs_ref[1, s, r * rows4:(r + 1) * rows4, :] = x4
            for cls in range(16):
                x16 = xs_ref[1, s, pl.ds((cls % 4) * rows4 + cls // 4, rows16, stride=4), :]
                h16_ref[pt, cls * rows16:(cls + 1) * rows16, lanes] = x16.astype(BF16)

    def stage_v(pt):
        for cb in range(a_width // cbw):
            s_ref[pt, :, cb * cbw:(cb + 1) * cbw] = jax.nn.gelu(proj(pt, col_v + cb * cbw, cbw))

    def emit_q(q, q_ref, dil, pt):
        rows = th // Q_ORDER
        if dil == Q_ORDER:
            _emit_residue_major(q, q_ref, pt, n_parts)
        elif dil == 1:
            per_blk = ATTN_BLOCK // Q_ORDER
            for hp in range(N_PAIRS):
                for blk in range(th // ATTN_BLOCK):
                    for a in range(0, Q_ORDER, 2):
                        pair = [q[c * rows + blk * per_blk:c * rows + (blk + 1) * per_blk,
                                  hp * LANES:(hp + 1) * LANES] for c in (a, a + 1)]
                        dst = pt * th + blk * ATTN_BLOCK + a * per_blk
                        q_ref[hp, 0, dst:dst + 2 * per_blk, :] = jnp.concatenate(pair, axis=0).astype(q_ref.dtype)
        else:
            assert tm == ATTN_BLOCK * dil
            per_tile = tm // Q_ORDER
            for hp in range(N_PAIRS):
                for cls in range(Q_ORDER):
                    dst = (cls // dil) * per_tile + pt * rows
                    q_ref[hp, cls % dil, dst:dst + rows, :] = (
                        q[cls * rows:(cls + 1) * rows, hp * LANES:(hp + 1) * LANES].astype(q_ref.dtype))

    def stage_qkv(pt, p):
        q_ref, k_ref, v_ref = outs[p]
        dil = B_PATTERNS[p][1]
        hsrc = h_by_dilation[dil]
        base = col_qkv + p * 3 * B_WIDTH
        emit_q(_head_rms(proj(pt, base, B_WIDTH, h_by_dilation[Q_ORDER]), gq_ref[p] * Q_SCALE), q_ref, dil, pt)
        _emit_residue_major(_head_rms(proj(pt, base + B_WIDTH, B_WIDTH, hsrc), gk_ref[p]), k_ref, pt, n_parts)
        _emit_residue_major(proj(pt, base + 2 * B_WIDTH, B_WIDTH, hsrc), v_ref, pt, n_parts)

    def stage_layernorm(pt):
        v = s_ref[pt]
        mu = jnp.mean(v, axis=-1, keepdims=True)
        vc = v - mu
        var = jnp.mean(vc * vc, axis=-1, keepdims=True)
        vb_ref[pt] = (vc * lax.rsqrt(var + EPS) * lng_ref[...] + lnb_ref[...]).astype(BF16)

    def stage_spatial(pt):
        for g in range(A_GROUPS):
            w = jnp.where(row >= col, ws_ref[g], 0.0).astype(BF16)
            gs = slice(g * gw, (g + 1) * gw)
            for c in range(th // CHUNK):
                rs = slice(c * CHUNK, (c + 1) * CHUNK)
                s_ref[pt, rs, gs] = (jnp.dot(w, vb_ref[pt, rs, gs], preferred_element_type=F32)
                                     + sb_ref[:, gs])

    def stage_bgate(pt):
        bg = jax.nn.silu(proj(pt, col_bg, B_WIDTH))
        for hp in range(N_PAIRS):
            bg_ref[hp, rows_of(pt), :] = bg[:, hp * LANES:(hp + 1) * LANES].astype(bg_ref.dtype)

    def stage_gates(pt):
        for cb in range(a_width // cbw):
            cs = slice(cb * cbw, (cb + 1) * cbw)
            u = jax.nn.gelu(proj(pt, col_u + cb * cbw, cbw))
            gate = jax.nn.silu(proj(pt, col_g + cb * cbw, cbw))
            yp_ref[pt, :, cs] = (u * s_ref[pt, :, cs] * gate).astype(BF16)

    def stage_out(pt):
        for cb in range(d_model // cbw):
            cs = slice(cb * cbw, (cb + 1) * cbw)
            gb_ref[rows_of(pt), cs] = jax.nn.sigmoid(proj(pt, col_gb + cb * cbw, cbw)).astype(gb_ref.dtype)
            ya = jnp.dot(yp_ref[pt], woa_bf_ref[:, cs], preferred_element_type=F32)
            za_ref[rows_of(pt), cs] = (jax.nn.sigmoid(proj(pt, col_ga + cb * cbw, cbw)) * ya).astype(za_ref.dtype)

    stages = (stage_norm, stage_v, lambda pt: stage_qkv(pt, 0), stage_layernorm,
              lambda pt: stage_qkv(pt, 1), stage_spatial,
              lambda pt: (stage_qkv(pt, 2), stage_bgate(pt)), stage_gates, stage_out)
    for k in range(len(stages) + n_parts - 1):
        for pt in range(n_parts):
            if 0 <= k - pt < len(stages):
                stages[k - pt](pt)


def _fused_in(x2, ng, win, ws, sb, lng, lnb, woa, gq, gk, *, tm=512):
    t, d = x2.shape
    a_width = woa.shape[0]
    parts, th = FUSED_IN_PARTS, tm // FUSED_IN_PARTS
    row = lambda width: pl.BlockSpec((tm, width), lambda i: (i, 0))
    hbm = pl.BlockSpec(memory_space=pl.ANY)
    out_specs = [row(d), row(d), pl.BlockSpec((N_PAIRS, tm, LANES), lambda i: (0, i, 0))]
    out_shape = [jax.ShapeDtypeStruct((t, d), BF16), jax.ShapeDtypeStruct((t, d), BF16),
                 jax.ShapeDtypeStruct((N_PAIRS, t, LANES), BF16)]
    for _, dil in B_PATTERNS:
        spec = pl.BlockSpec((N_PAIRS, dil, tm // dil, LANES), lambda i: (0, 0, i, 0))
        shape = jax.ShapeDtypeStruct((N_PAIRS, dil, t // dil, LANES), BF16)
        out_specs += [spec] * 3
        out_shape += [shape] * 3
    return pl.pallas_call(
        _fused_in_kernel,
        grid=(t // tm,),
        in_specs=[row(d), _resident(ng.shape), hbm, _resident(ws.shape),
                  _resident(sb.shape), _resident(lng.shape), _resident(lnb.shape), hbm,
                  _resident(gq.shape), _resident(gk.shape)],
        out_specs=out_specs,
        out_shape=out_shape,
        scratch_shapes=[pltpu.VMEM((parts, th, d), BF16), pltpu.VMEM((parts, th, d), BF16),
                        pltpu.VMEM((parts, th, d), BF16), pltpu.VMEM((2, d // LANES, th, LANES), F32),
                        pltpu.VMEM((parts, th, a_width), F32), pltpu.VMEM((parts, th, a_width), BF16),
                        pltpu.VMEM((parts, th, a_width), BF16),
                        pltpu.VMEM(win.shape, BF16), pltpu.VMEM(woa.shape, BF16)]
        + _weight_scratch(win.shape) + _weight_scratch(woa.shape),
        compiler_params=_params("arbitrary"),
        name="fused_in",
    )(x2, ng, win, ws, sb, lng, lnb, woa, gq, gk)


def _attn_unit(q2, k2, v2, bias, low):
    n = ATTN_BLOCK
    v2e = jnp.concatenate([v2, jnp.ones_like(v2)], axis=1)
    zero = jnp.zeros_like(q2)
    qs = jnp.concatenate([jnp.where(low, q2, zero), jnp.where(low, zero, q2)], axis=0)
    s = lax.dot_general(qs, k2, (((1,), (1,)), ((), ())), preferred_element_type=F32)
    s = s + jnp.concatenate([bias, bias], axis=0)
    m = jnp.max(s, axis=-1, keepdims=True)
    e = jnp.exp2(s - m).astype(BF16)
    oe = jnp.dot(e, v2e, preferred_element_type=F32)
    mb = jnp.broadcast_to(m, (2 * n, LANES))
    return (jnp.where(low, oe[:n, :LANES], oe[n:, :LANES]),
            jnp.where(low, oe[:n, LANES:], oe[n:, LANES:]),
            jnp.where(low, mb[:n], mb[n:]))


def _attn_out_kernel(q0_ref, k0_ref, v0_ref, kp0_ref, vp0_ref, q1_ref, k1_ref, v1_ref, kp1_ref, vp1_ref,
                     q2_ref, k2_ref, v2_ref, kp2_ref, vp2_ref, bias_ref, bg_ref,
                     gb_ref, za_ref, x_ref, wob_ref, wout_ref, out_ref,
                     on_ref, lse_ref, ybn_ref, yb_ref, wob_bf_ref, wout_bf_ref, stage_ref, sem,
                     *, n_spans, spans_per_seq):
    s = pl.program_id(0)
    hp = pl.program_id(1)
    tq = out_ref.shape[0]

    @pl.when(jnp.logical_and(s == 0, hp == 0))
    def _():
        yb_ref[...] = jnp.zeros_like(yb_ref)
        _load_weight_bf16(wob_ref, wob_bf_ref, stage_ref, sem)
        _load_weight_bf16(wout_ref, wout_bf_ref, stage_ref, sem)

    def out_projection(lo, n):
        src_rows = pl.ds(pl.multiple_of(hp * tq + lo, n), n)
        yb = jnp.concatenate([yb_ref[(s + 1) % 2, j, src_rows, :] for j in range(N_PAIRS)], axis=1)
        yb = jnp.dot(yb, wob_bf_ref[...], preferred_element_type=F32)
        rs = slice(lo, lo + n)
        merged = za_ref[rs, :].astype(F32) + gb_ref[rs, :].astype(F32) * yb
        out_ref[rs, :] = x_ref[rs, :] + jnp.dot(merged.astype(BF16), wout_bf_ref[...],
                                                 preferred_element_type=F32)

    @pl.when(s == n_spans)
    def _():
        out_projection(0, tq)

    @pl.when(s < n_spans)
    def _():
        first_span = s % spans_per_seq == 0
        low = lax.broadcasted_iota(jnp.int32, (1, LANES), 1) < B_HEAD_DIM
        pats = ((q0_ref, k0_ref, v0_ref, kp0_ref, vp0_ref),
                (q1_ref, k1_ref, v1_ref, kp1_ref, vp1_ref),
                (q2_ref, k2_ref, v2_ref, kp2_ref, vp2_ref))
        def unit(p, r, c):
            q_ref, k_ref, v_ref, kp_ref, vp_ref = pats[p]
            dil = B_PATTERNS[p][1]
            rows = slice(c * ATTN_BLOCK, (c + 1) * ATTN_BLOCK)
            if c == 0:
                k2 = jnp.concatenate([kp_ref[r], k_ref[r, rows, :]], axis=0)
                v2 = jnp.concatenate([vp_ref[r], v_ref[r, rows, :]], axis=0)
                bias = bias_ref[p, jnp.where(first_span, 0, 1)]
            else:
                both = slice((c - 1) * ATTN_BLOCK, (c + 1) * ATTN_BLOCK)
                k2, v2 = k_ref[r, both, :], v_ref[r, both, :]
                bias = bias_ref[p, 1]
            num, den, mx = _attn_unit(q_ref[r, rows, :], k2, v2, bias, low)
            tiles = (num / den, mx + jnp.log2(den))
            groups = Q_ORDER // dil
            g_rows = ATTN_BLOCK // groups
            for g in range(groups):
                dst0 = (r + dil * g) * CLASS_ROWS + c * g_rows
                for ref, tile in zip((on_ref, lse_ref), tiles):
                    ref[p, dst0:dst0 + g_rows, :] = tile[g * g_rows:(g + 1) * g_rows]

        def combine(cls):
            rs = slice(cls * CLASS_ROWS, (cls + 1) * CLASS_ROWS)
            l0, l1, l2 = lse_ref[0, rs], lse_ref[1, rs], lse_ref[2, rs]
            mm = jnp.maximum(jnp.maximum(l0, l1), l2)
            a0, a1, a2 = jnp.exp2(l0 - mm), jnp.exp2(l1 - mm), jnp.exp2(l2 - mm)
            num = a0 * on_ref[0, rs] + a1 * on_ref[1, rs] + a2 * on_ref[2, rs]
            ybn_ref[pl.ds(cls, CLASS_ROWS, stride=Q_ORDER), :] = num / (a0 + a1 + a2)

        p16, p4, p1 = 2, 1, 0
        assert [B_PATTERNS[p][1] for p in (p16, p4, p1)] == [16, 4, 1]
        for r in range(16):
            unit(p16, r, 0)
        for c4 in range(UNITS // 4):
            for r in range(4):
                unit(p4, r, c4)
        for c in range(UNITS):
            unit(p1, 0, c)
        parts = 2
        for part in range(parts):
            out_projection(part * (tq // parts), tq // parts)
            for cls in range(part * (Q_ORDER // parts), (part + 1) * (Q_ORDER // parts)):
                combine(cls)
        rows = 256
        for i in range(SPAN // rows):
            rs = slice(i * rows, (i + 1) * rows)
            yb_ref[s % 2, hp, rs, :] = (ybn_ref[rs, :] * bg_ref[rs, :].astype(F32)).astype(yb_ref.dtype)


def _attn_out(qkv, bias, bg, gb, za, x2, wob, wout, *, bsz, seq):
    spans = seq // SPAN
    n_spans = bsz * spans
    t, d = x2.shape
    tq = SPAN // N_PAIRS

    in_specs = []
    for _, dil in B_PATTERNS:
        rows = SPAN // dil

        def cur_map(s, hp):
            return (hp, 0, jnp.minimum(s, n_spans - 1), 0)

        def prev_map(s, hp, rows=rows):
            return (hp, 0, jnp.maximum(jnp.minimum(s, n_spans - 1) * (rows // ATTN_BLOCK) - 1, 0), 0)

        cur = pl.BlockSpec((None, dil, rows, LANES), cur_map)
        prev = pl.BlockSpec((None, dil, ATTN_BLOCK, LANES), prev_map)
        in_specs += [cur, cur, cur, prev, prev]
    tail = pl.BlockSpec((tq, d), lambda s, hp: (jnp.maximum((s - 1) * N_PAIRS + hp, 0), 0))
    in_specs += [pl.BlockSpec(bias.shape, lambda s, hp: (0,) * bias.ndim),
                 pl.BlockSpec((None, SPAN, LANES), lambda s, hp: (hp, jnp.minimum(s, n_spans - 1), 0)),
                 tail, tail, tail, pl.BlockSpec(memory_space=pl.ANY), pl.BlockSpec(memory_space=pl.ANY)]
    args = []
    for q, k, v in qkv:
        args += [q, k, v, k, v]
    return pl.pallas_call(
        functools.partial(_attn_out_kernel, n_spans=n_spans, spans_per_seq=spans),
        grid=(n_spans + 1, N_PAIRS),
        in_specs=in_specs,
        out_specs=tail,
        out_shape=jax.ShapeDtypeStruct((t, d), F32),
        scratch_shapes=[pltpu.VMEM((len(B_PATTERNS), SPAN, LANES), F32)] * 2
        + [pltpu.VMEM((SPAN, LANES), F32), pltpu.VMEM((2, N_PAIRS, SPAN, LANES), BF16), pltpu.VMEM(wob.shape, BF16), pltpu.VMEM(wout.shape, BF16)]
        + _weight_scratch(wout.shape),
        compiler_params=_params("arbitrary", "arbitrary"),
        name="attn_out",
    )(*args, bias, bg, gb, za, x2, wob, wout)


def _band_bias():
    kj = lax.broadcasted_iota(jnp.int32, (ATTN_BLOCK, 2 * ATTN_BLOCK), 1)
    row = lax.broadcasted_iota(jnp.int32, (ATTN_BLOCK, 2 * ATTN_BLOCK), 0)
    out = []
    for _, dil in B_PATTERNS:
        groups = Q_ORDER // dil
        g_rows = ATTN_BLOCK // groups
        qi = (row % g_rows) * groups + row // g_rows
        dist = qi + ATTN_BLOCK - kj
        band = (dist >= 0) & (dist <= ATTN_BLOCK)
        first = band & (kj >= ATTN_BLOCK)
        out.append(jnp.stack([jnp.where(first, 0.0, NEG), jnp.where(band, 0.0, NEG)]))
    return jnp.stack(out).astype(F32)


def kernel(x, norm_g, w_in, a_ws, a_bs, a_ln_g, a_ln_b, b_qn_g, b_kn_g, w_oa, w_ob, w_out):
    bsz, seq, d = x.shape
    depth = w_in.shape[0]
    a_width = w_oa.shape[1]
    npat = len(B_PATTERNS)
    assert w_in.shape[2] == 3 * a_width + 3 * npat * B_WIDTH + B_WIDTH + 2 * d
    assert all(w // dil == ATTN_BLOCK and SPAN % (ATTN_BLOCK * dil) == 0 for w, dil in B_PATTERNS)
    assert seq % SPAN == 0
    t = bsz * seq
    band_bias = _band_bias()
    x2 = x.reshape(t, d)
    for l in range(depth):
        sgu_bias = jnp.repeat(a_bs[l].T, a_width // A_GROUPS, axis=1)
        gq = jnp.tile(b_qn_g[l], (1, B_HEADS)).reshape(npat, 1, B_WIDTH)
        gk = jnp.tile(b_kn_g[l], (1, B_HEADS)).reshape(npat, 1, B_WIDTH)
        res = _fused_in(x2, norm_g[l].reshape(1, d), w_in[l], a_ws[l], sgu_bias,
                        a_ln_g[l].reshape(1, -1), a_ln_b[l].reshape(1, -1), w_oa[l],
                        gq, gk)
        za, gb, bg = res[:3]
        qkv = [res[3 + 3 * p:6 + 3 * p] for p in range(npat)]
        x2 = _attn_out(qkv, band_bias, bg, gb, za, x2, w_ob[l], w_out[l],
                       bsz=bsz, seq=seq)
    return x2.reshape(bsz, seq, d)
```
